```python
import jax, jax.numpy as jnp
from jax import lax
import numpy as np

D_MODEL = 1024
BATCH = 32
SEQ = 256
DEPTH = 4
DEC_BATCH = 2
DEC_SEQ = 2048
PAST_LEN = 512

GRID_W = 64
HEAD_DIM = 64
ATTN_WIDTH = D_MODEL // 2
N_HEADS = ATTN_WIDTH // HEAD_DIM
N_KV_HEADS = 2
GROUP = N_HEADS // N_KV_HEADS
KV_WIDTH = N_KV_HEADS * HEAD_DIM
WINDOW = 128
BLOCK = 128
GM_WIDTH = D_MODEL // 4
GM_HEADS = 4
GM_DIM = GM_WIDTH // GM_HEADS
CHUNK = 128
POOL_WIDTH = D_MODEL // 4
POOL_WINDOWS = (2, 4, 8, 16)
POOL_DIM = POOL_WIDTH // len(POOL_WINDOWS)
MIX_WIDTH = ATTN_WIDTH + GM_WIDTH + POOL_WIDTH
IN_WIDTH = ATTN_WIDTH + 2 * KV_WIDTH + 2 * GM_WIDTH + POOL_WIDTH
SPLITS = (ATTN_WIDTH, ATTN_WIDTH + KV_WIDTH, ATTN_WIDTH + 2 * KV_WIDTH,
          ATTN_WIDTH + 2 * KV_WIDTH + GM_WIDTH, ATTN_WIDTH + 2 * KV_WIDTH + 2 * GM_WIDTH)
D_FF = 2816
N_MOD = 9
EPS = 1e-6
ROPE_BASE = 10000.0
NEG_INF = -1e30
ATTN_SCALE = HEAD_DIM ** -0.5

kernel_name = "hymba_macaron_prefix_dit_step"


def rms_norm(x, g):
    xf = x.astype(jnp.float32)
    y = xf * lax.rsqrt(jnp.mean(xf * xf, axis=-1, keepdims=True) + EPS)
    return (y * g.astype(jnp.float32)).astype(x.dtype)


def adaln(cvec, w_mod_l, b_mod_l):
    m = jax.nn.silu(cvec) @ w_mod_l + b_mod_l
    return m.reshape(cvec.shape[0], N_MOD, D_MODEL)


def modulate(h, mod, j):
    return h * (1 + mod[:, j + 1][:, None, :]) + mod[:, j][:, None, :]


def swiglu(h, w1, w2):
    g, u = jnp.split(h @ w1, 2, axis=-1)
    return (jax.nn.silu(g) * u) @ w2


def ffn_half(x, mod, j, g_pre, g_post, w1, w2):
    h = modulate(rms_norm(x, g_pre), mod, j)
    return x + 0.5 * mod[:, j + 2][:, None, :] * rms_norm(swiglu(h, w1, w2), g_post)


def axial_rope_tables(n_tok):
    rows = n_tok // GRID_W
    t = jnp.arange(rows * GRID_W, dtype=jnp.int32)
    row = (t // GRID_W).astype(jnp.float32)
    col = (t % GRID_W).astype(jnp.float32)
    nf = HEAD_DIM // 4
    inv = ROPE_BASE ** (-jnp.arange(nf, dtype=jnp.float32) / nf)
    ang = jnp.concatenate([row[:, None] * inv, col[:, None] * inv], axis=-1)
    return jnp.cos(ang), jnp.sin(ang)


def apply_axial_rope(x, cos, sin):
    S = x.shape[1]
    nf = HEAD_DIM // 4
    shape = (1, S) + (1,) * (x.ndim - 3) + (nf,)
    out = []
    for a in range(2):
        z = x[..., a * 2 * nf:(a + 1) * 2 * nf]
        z1, z2 = z[..., :nf], z[..., nf:]
        cs = cos[:, a * nf:(a + 1) * nf].reshape(shape).astype(x.dtype)
        sn = sin[:, a * nf:(a + 1) * nf].reshape(shape).astype(x.dtype)
        out += [z1 * cs - z2 * sn, z2 * cs + z1 * sn]
    return jnp.concatenate(out, axis=-1)


def sink_column(sink, lead_shape):
    s = sink.astype(jnp.float32).reshape(N_KV_HEADS, GROUP)[None, :, :, None, None]
    return jnp.broadcast_to(s, lead_shape + (1,))


def project(h, w_in):
    B, S, _ = h.shape
    q, k, v, gu, gv, pl = jnp.split(h @ w_in, SPLITS, axis=-1)
    q = q.reshape(B, S, N_KV_HEADS, GROUP, HEAD_DIM)
    k = k.reshape(B, S, N_KV_HEADS, HEAD_DIM)
    v = v.reshape(B, S, N_KV_HEADS, HEAD_DIM)
    return q, k, v, gu, gv, pl


def context_attention(q, k, v, sink):
    B, S = q.shape[:2]
    nblk = S // BLOCK
    qb = q.reshape(B, nblk, BLOCK, N_KV_HEADS, GROUP, HEAD_DIM).swapaxes(0, 1)

    def one(qblk):
        s = jnp.einsum('bqkgd,bskd->bkgqs', qblk, k).astype(jnp.float32) * ATTN_SCALE
        p = jax.nn.softmax(jnp.concatenate([s, sink_column(sink, s.shape[:-1])], axis=-1), axis=-1)
        return jnp.einsum('bkgqs,bskd->bqkgd', p[..., :-1].astype(v.dtype), v)

    o = lax.map(one, qb)
    return o.swapaxes(0, 1).reshape(B, S, ATTN_WIDTH)


def latent_attention(q, k, v, ck, cv, sink):
    B, S = q.shape[:2]
    nblk = S // BLOCK
    pad = ((0, 0), (BLOCK, BLOCK), (0, 0), (0, 0))
    kp = jnp.pad(k, pad)
    vp = jnp.pad(v, pad)
    qb = q.reshape(B, nblk, BLOCK, N_KV_HEADS, GROUP, HEAD_DIM).swapaxes(0, 1)
    qoff = jnp.arange(BLOCK, dtype=jnp.int32)
    koff = jnp.arange(3 * BLOCK, dtype=jnp.int32)
    n_band = 3 * BLOCK

    def one(args):
        i, qblk = args
        kb = lax.dynamic_slice_in_dim(kp, i * BLOCK, n_band, axis=1)
        vb = lax.dynamic_slice_in_dim(vp, i * BLOCK, n_band, axis=1)
        qpos = i * BLOCK + qoff
        kpos = i * BLOCK - BLOCK + koff
        valid = (kpos[None, :] >= 0) & (kpos[None, :] < S) & (jnp.abs(qpos[:, None] - kpos[None, :]) <= WINDOW)
        s_lat = jnp.einsum('bqkgd,bskd->bkgqs', qblk, kb).astype(jnp.float32) * ATTN_SCALE
        s_lat = jnp.where(valid, s_lat, NEG_INF)
        s_ctx = jnp.einsum('bqkgd,bckd->bkgqc', qblk, ck).astype(jnp.float32) * ATTN_SCALE
        p = jax.nn.softmax(jnp.concatenate([s_lat, s_ctx, sink_column(sink, s_lat.shape[:-1])], axis=-1), axis=-1)
        o = jnp.einsum('bkgqs,bskd->bqkgd', p[..., :n_band].astype(vb.dtype), vb)
        return o + jnp.einsum('bkgqc,bckd->bqkgd', p[..., n_band:-1].astype(cv.dtype), cv)

    o = lax.map(one, (jnp.arange(nblk, dtype=jnp.int32), qb))
    return o.swapaxes(0, 1).reshape(B, S, ATTN_WIDTH)


def chunk_gmlp(u, v, w_s, b_s):
    B, S, _ = v.shape
    n = S // CHUNK
    vh = v.reshape(B, n, CHUNK, GM_HEADS, GM_DIM).astype(jnp.float32)
    vh = (vh * lax.rsqrt(jnp.mean(vh * vh, axis=-1, keepdims=True) + EPS)).astype(v.dtype)
    z = jnp.einsum('hpq,bnqhd->bnphd', w_s, vh) + b_s.T[None, None, :, :, None]
    return u * z.reshape(B, S, GM_WIDTH)


def multiscale_pool(h, w_pool, scale):
    B, S, C = h.shape
    hf = h.astype(jnp.float32)
    cs = jnp.concatenate([jnp.zeros((B, 1, C), jnp.float32), jnp.cumsum(hf, axis=1)], axis=1)
    t = jnp.arange(S, dtype=jnp.int32)
    outs = []
    for g, w in enumerate(POOL_WINDOWS):
        lo = jnp.clip(t - w // 2, 0, S)
        hi = jnp.clip(t + w // 2, 0, S)
        cg = cs[..., g * POOL_DIM:(g + 1) * POOL_DIM]
        mean = (cg[:, hi] - cg[:, lo]) / (hi - lo).astype(jnp.float32)[None, :, None]
        d = (mean - hf[..., g * POOL_DIM:(g + 1) * POOL_DIM]).astype(h.dtype)
        outs.append(d @ w_pool[g])
    return jnp.concatenate(outs, axis=-1) * scale


def merge_heads(attn, gu, gv, pl, w_s, b_s, w_pool, pool_scale, w_out):
    a = chunk_gmlp(gu, gv, w_s, b_s)
    p = multiscale_pool(pl, w_pool, pool_scale)
    return jnp.concatenate([attn, a, p], axis=-1) @ w_out


def setup_inputs(seed: int = 0) -> dict:
    key = jax.random.key(seed)
    ks = jax.random.split(key, 18)
    f = jnp.float32
    nrm = lambda k, shape, s: jax.random.normal(k, shape, f) * s
    return {
        "x_prompt": nrm(ks[0], (BATCH, SEQ, D_MODEL), 1.0),
        "x_sample": nrm(ks[1], (DEC_BATCH, DEC_SEQ, D_MODEL), 1.0),
        "cache_k": nrm(ks[2], (DEC_BATCH, DEPTH, PAST_LEN, N_KV_HEADS, HEAD_DIM), 1.0),
        "cache_v": nrm(ks[3], (DEC_BATCH, DEPTH, PAST_LEN, N_KV_HEADS, HEAD_DIM), 1.0),
        "c": nrm(ks[4], (DEC_BATCH, D_MODEL), 1.0),
        "c_ctx": nrm(ks[5], (D_MODEL,), 1.0),
        "w_mod": nrm(ks[6], (DEPTH, D_MODEL, N_MOD * D_MODEL), 0.5 * D_MODEL ** -0.5),
        "b_mod": nrm(ks[7], (DEPTH, N_MOD * D_MODEL), 0.02),
        "norm_w": 1.0 + nrm(ks[8], (DEPTH, 6, D_MODEL), 0.02),
        "w_in": nrm(ks[9], (DEPTH, D_MODEL, IN_WIDTH), D_MODEL ** -0.5),
        "w_out": nrm(ks[10], (DEPTH, MIX_WIDTH, D_MODEL), MIX_WIDTH ** -0.5),
        "attn_sink": nrm(ks[11], (DEPTH, N_HEADS), 0.5),
        "w_spatial": nrm(ks[12], (DEPTH, GM_HEADS, CHUNK, CHUNK), CHUNK ** -0.5),
        "b_spatial": 1.0 + nrm(ks[13], (DEPTH, GM_HEADS, CHUNK), 0.02),
        "w_pool": nrm(ks[14], (DEPTH, len(POOL_WINDOWS), POOL_DIM, POOL_DIM), POOL_DIM ** -0.5),
        "pool_scale": 1.0 + nrm(ks[15], (DEPTH, POOL_WIDTH), 0.02),
        "ffn_w1": nrm(ks[16], (DEPTH, 2, D_MODEL, 2 * D_FF), D_MODEL ** -0.5),
        "ffn_w2": nrm(ks[17], (DEPTH, 2, D_FF, D_MODEL), D_FF ** -0.5),
    }


def reference(x_prompt, x_sample, cache_k, cache_v, c, c_ctx, w_mod, b_mod, norm_w, w_in, w_out,
              attn_sink, w_spatial, b_spatial, w_pool, pool_scale, ffn_w1, ffn_w2):
    y = x_prompt
    new_k, new_v = [], []
    for l in range(DEPTH):
        mod = adaln(c_ctx[None, :], w_mod[l], b_mod[l])
        y = ffn_half(y, mod, 0, norm_w[l, 0], norm_w[l, 1], ffn_w1[l, 0], ffn_w2[l, 0])
        h = modulate(rms_norm(y, norm_w[l, 2]), mod, 3)
        q, k, v, gu, gv, pl = project(h, w_in[l])
        attn = context_attention(q, k, v, attn_sink[l])
        out = merge_heads(attn, gu, gv, pl, w_spatial[l], b_spatial[l], w_pool[l], pool_scale[l], w_out[l])
        y = y + mod[:, 5][:, None, :] * rms_norm(out, norm_w[l, 3])
        y = ffn_half(y, mod, 6, norm_w[l, 4], norm_w[l, 5], ffn_w1[l, 1], ffn_w2[l, 1])
        new_k.append(k)
        new_v.append(v)
    y_prompt = y
    new_cache_k = jnp.stack(new_k, axis=1)
    new_cache_v = jnp.stack(new_v, axis=1)

    cos, sin = axial_rope_tables(x_sample.shape[1])
    z = x_sample
    for l in range(DEPTH):
        mod = adaln(c, w_mod[l], b_mod[l])
        z = ffn_half(z, mod, 0, norm_w[l, 0], norm_w[l, 1], ffn_w1[l, 0], ffn_w2[l, 0])
        h = modulate(rms_norm(z, norm_w[l, 2]), mod, 3)
        q, k, v, gu, gv, pl = project(h, w_in[l])
        q = apply_axial_rope(q, cos, sin)
        k = apply_axial_rope(k, cos, sin)
        attn = latent_attention(q, k, v, cache_k[:, l], cache_v[:, l], attn_sink[l])
        out = merge_heads(attn, gu, gv, pl, w_spatial[l], b_spatial[l], w_pool[l], pool_scale[l], w_out[l])
        z = z + mod[:, 5][:, None, :] * rms_norm(out, norm_w[l, 3])
        z = ffn_half(z, mod, 6, norm_w[l, 4], norm_w[l, 5], ffn_w1[l, 1], ffn_w2[l, 1])
    y_sample = z
    return (y_prompt, y_sample, new_cache_k, new_cache_v)
```

```python
import functools

import jax
import jax.numpy as jnp
from jax import lax
from jax.experimental import pallas as pl
from jax.experimental.pallas import tpu as pltpu

D_MODEL = 1024
N_CTX_SEQ = 32
CTX_SEQ = 256
DEPTH = 4
N_LAT_SEQ = 2
LAT_SEQ = 2048
PAST_LEN = 512
GRID_W = 64
HEAD_DIM = 64
ATTN_WIDTH = 512
N_HEADS = 8
N_KV_HEADS = 2
KV_WIDTH = 128
WINDOW = 128
GM_WIDTH = 256
GM_HEADS = 4
GM_DIM = 64
CHUNK = 128
POOL_WIDTH = 256
POOL_HALF_WINDOWS = (1, 2, 4, 8)
POOL_DIM = 64
IN_WIDTH = 1536
D_FF = 2816
N_MOD = 9
EPS = 1e-6
ROPE_BASE = 10000.0
NEG_INF = -1e30
ATTN_SCALE = HEAD_DIM ** -0.5

T_CTX = N_CTX_SEQ * CTX_SEQ
T_LAT = N_LAT_SEQ * LAT_SEQ
T_ALL = T_CTX + T_LAT
N_COND = 8

LANES = 128
VMEM_LIMIT = 56 * 1024 * 1024

FFN_TM = 1024
FFN_CHUNKS = (512, 512, 512, 512, 512, 256)
PROJ_TM = 512
Q_BLK = 256
BAND = Q_BLK + 2 * WINDOW
MOD_TN = 2304

BF16 = jnp.bfloat16
F32 = jnp.float32


def _dot(a, b):
    return jnp.dot(a, b, preferred_element_type=F32)


def _dot_t(a, b):
    return lax.dot_general(a, b, (((1,), (1,)), ((), ())), preferred_element_type=F32)


def _rms(x, g):
    return x * lax.rsqrt(jnp.mean(x * x, axis=-1, keepdims=True) + EPS) * g


def _cond_row(tile, tm):
    n_ctx = T_CTX // tm
    per_seq = LAT_SEQ // tm
    return jnp.where(tile < n_ctx, 0, 1 + (tile - n_ctx) // per_seq)


def _mod_kernel(c_ref, w_ref, b_ref, o_ref):
    c = c_ref[...]
    s = jax.nn.silu(c).astype(BF16)
    o_ref[...] = _dot(s, w_ref[...].astype(BF16)) + b_ref[...]


def _adaln(cond, w_mod, b_mod):
    n_out = N_MOD * D_MODEL
    return pl.pallas_call(
        _mod_kernel,
        out_shape=jax.ShapeDtypeStruct((DEPTH, N_COND, n_out), F32),
        grid=(DEPTH, n_out // MOD_TN),
        in_specs=[
            pl.BlockSpec((N_COND, D_MODEL), lambda l, n: (0, 0)),
            pl.BlockSpec((None, D_MODEL, MOD_TN), lambda l, n: (l, 0, n)),
            pl.BlockSpec((None, 1, MOD_TN), lambda l, n: (l, 0, n)),
        ],
        out_specs=pl.BlockSpec((None, N_COND, MOD_TN), lambda l, n: (l, 0, n)),
        compiler_params=pltpu.CompilerParams(
            dimension_semantics=("arbitrary", "arbitrary"), vmem_limit_bytes=VMEM_LIMIT),
        name="adaln",
    )(cond, w_mod, b_mod.reshape(DEPTH, 1, n_out))


def _ffn_kernel(x_ref, mod_ref, nw_ref, w1_ref, w2_ref, o_ref, *, j, pre, post):
    x = x_ref[...]
    shift = mod_ref[0, j:j + 1, :]
    scale = mod_ref[0, j + 1:j + 2, :]
    gate = mod_ref[0, j + 2:j + 3, :]
    h = (_rms(x, nw_ref[pre:pre + 1, :]) * (1 + scale) + shift).astype(BF16)
    acc = None
    off = 0
    for tk in FFN_CHUNKS:
        g = _dot(h, w1_ref[:, off:off + tk])
        u = _dot(h, w1_ref[:, D_FF + off:D_FF + off + tk])
        a = (jax.nn.silu(g) * u).astype(BF16)
        part = _dot(a, w2_ref[off:off + tk, :])
        acc = part if acc is None else acc + part
        off += tk
    o_ref[...] = x + 0.5 * gate * _rms(acc, nw_ref[post:post + 1, :])


def _ffn_half(x, mod_all, norm_w, w1, w2, l, half):
    j, pre, post = (0, 0, 1) if half == 0 else (6, 4, 5)
    tm = FFN_TM
    return pl.pallas_call(
        functools.partial(_ffn_kernel, j=j, pre=pre, post=post),
        out_shape=jax.ShapeDtypeStruct((T_ALL, D_MODEL), F32),
        grid=(T_ALL // tm,),
        in_specs=[
            pl.BlockSpec((tm, D_MODEL), lambda i: (i, 0)),
            pl.BlockSpec((None, 1, N_MOD, D_MODEL), lambda i: (l, _cond_row(i, tm), 0, 0)),
            pl.BlockSpec((None, 6, D_MODEL), lambda i: (l, 0, 0)),
            pl.BlockSpec((None, None, D_MODEL, 2 * D_FF), lambda i: (l, half, 0, 0),
                         pipeline_mode=pl.Buffered(1)),
            pl.BlockSpec((None, None, D_FF, D_MODEL), lambda i: (l, half, 0, 0),
                         pipeline_mode=pl.Buffered(1)),
        ],
        out_specs=pl.BlockSpec((tm, D_MODEL), lambda i: (i, 0)),
        compiler_params=pltpu.CompilerParams(
            dimension_semantics=("arbitrary",), vmem_limit_bytes=VMEM_LIMIT),
        name="ffn_half",
    )(x, mod_all, norm_w, w1, w2)


def _swap16(x):
    w = x.shape[-1]
    lane = lax.broadcasted_iota(jnp.int32, x.shape, x.ndim - 1)
    first = (lane & 31) < 16
    return jnp.where(first, pltpu.roll(x, w - 16, axis=x.ndim - 1), pltpu.roll(x, 16, axis=x.ndim - 1))


def _rope(x, cos_t, sin_t):
    reps = x.shape[-1] // LANES
    c = jnp.concatenate([cos_t] * reps, axis=-1) if reps > 1 else cos_t
    s = jnp.concatenate([sin_t] * reps, axis=-1) if reps > 1 else sin_t
    return x * c + _swap16(x) * s


def _proj_kernel(x_ref, mod_ref, nw_ref, w_ref, cos_ref, sin_ref, ws_ref, bz_ref,
                 q_ref, k_ref, v_ref, a_ref, p_ref):
    i = pl.program_id(0)
    tm = x_ref.shape[0]
    x = x_ref[...]
    shift = mod_ref[0, 3:4, :]
    scale = mod_ref[0, 4:5, :]
    h = (_rms(x, nw_ref[2:3, :]) * (1 + scale) + shift).astype(BF16)
    proj = _dot(h, w_ref[...])
    q = proj[:, 0:ATTN_WIDTH] * ATTN_SCALE
    k = proj[:, ATTN_WIDTH:ATTN_WIDTH + KV_WIDTH]
    v_ref[...] = proj[:, ATTN_WIDTH + KV_WIDTH:ATTN_WIDTH + 2 * KV_WIDTH]
    gu = proj[:, 768:1024]
    gv = proj[:, 1024:1280]
    p_ref[...] = proj[:, 1280:1536]

    is_lat = i >= T_CTX // tm

    @pl.when(is_lat)
    def _():
        cos_t = cos_ref[...]
        sin_t = sin_ref[...]
        q_ref[...] = _rope(q, cos_t, sin_t).astype(BF16)
        k_ref[...] = _rope(k, cos_t, sin_t)

    @pl.when(jnp.logical_not(is_lat))
    def _():
        q_ref[...] = q.astype(BF16)
        k_ref[...] = k

    r = lax.broadcasted_iota(jnp.int32, (GM_WIDTH, GM_WIDTH), 0)
    c = lax.broadcasted_iota(jnp.int32, (GM_WIDTH, GM_WIDTH), 1)
    grp_mean = jnp.where((r >> 6) == (c >> 6), 1.0 / GM_DIM, 0.0).astype(BF16)
    sq = gv * gv
    sq_hi = sq.astype(BF16)
    sq_lo = (sq - sq_hi.astype(F32)).astype(BF16)
    ms = _dot(sq_hi, grp_mean) + _dot(sq_lo, grp_mean)
    vh = (gv * lax.rsqrt(ms + EPS)).astype(BF16)
    lane = lax.broadcasted_iota(jnp.int32, (CHUNK, GM_WIDTH), 1)
    ws = ws_ref[...]
    bz = bz_ref[...]
    for n in range(tm // CHUNK):
        vn = vh[n * CHUNK:(n + 1) * CHUNK, :]
        bd = jnp.concatenate(
            [jnp.where((lane >> 6) == hh, vn, jnp.zeros_like(vn)) for hh in range(GM_HEADS)], axis=0)
        z = _dot(ws, bd) + bz
        a_ref[n * CHUNK:(n + 1) * CHUNK, :] = (gu[n * CHUNK:(n + 1) * CHUNK, :] * z).astype(BF16)


def _mixer_proj(x, mod_all, norm_w, w_in, cos_t, sin_t, ws_cat, bz, l):
    tm = PROJ_TM
    n_ctx = T_CTX // tm
    per_seq = LAT_SEQ // tm

    def tbl_map(i):
        return (lax.rem(jnp.maximum(i - n_ctx, 0), per_seq), 0)

    row = lambda i: (i, 0)
    return pl.pallas_call(
        _proj_kernel,
        out_shape=(
            jax.ShapeDtypeStruct((T_ALL, ATTN_WIDTH), BF16),
            jax.ShapeDtypeStruct((T_ALL, KV_WIDTH), F32),
            jax.ShapeDtypeStruct((T_ALL, KV_WIDTH), F32),
            jax.ShapeDtypeStruct((T_ALL, GM_WIDTH), BF16),
            jax.ShapeDtypeStruct((T_ALL, POOL_WIDTH), F32),
        ),
        grid=(T_ALL // tm,),
        in_specs=[
            pl.BlockSpec((tm, D_MODEL), row),
            pl.BlockSpec((None, 1, N_MOD, D_MODEL), lambda i: (l, _cond_row(i, tm), 0, 0)),
            pl.BlockSpec((None, 6, D_MODEL), lambda i: (l, 0, 0)),
            pl.BlockSpec((None, D_MODEL, IN_WIDTH), lambda i: (l, 0, 0)),
            pl.BlockSpec((tm, LANES), tbl_map),
            pl.BlockSpec((tm, LANES), tbl_map),
            pl.BlockSpec((None, CHUNK, GM_HEADS * CHUNK), lambda i: (l, 0, 0)),
            pl.BlockSpec((None, CHUNK, GM_WIDTH), lambda i: (l, 0, 0)),
        ],
        out_specs=(
            pl.BlockSpec((tm, ATTN_WIDTH), row),
            pl.BlockSpec((tm, KV_WIDTH), row),
            pl.BlockSpec((tm, KV_WIDTH), row),
            pl.BlockSpec((tm, GM_WIDTH), row),
            pl.BlockSpec((tm, POOL_WIDTH), row),
        ),
        compiler_params=pltpu.CompilerParams(
            dimension_semantics=("arbitrary",), vmem_limit_bytes=VMEM_LIMIT),
        name="mixer_proj",
    )(x, mod_all, norm_w, w_in, cos_t, sin_t, ws_cat, bz)


def _pool(x, wp, scale):
    s_len = x.shape[0]
    row = lax.broadcasted_iota(jnp.int32, x.shape, 0)
    lane = lax.broadcasted_iota(jnp.int32, x.shape, 1)

    def down(y, s):
        return jnp.where(row >= s, pltpu.roll(y, s, axis=0), 0.0)

    def up(y, s):
        return jnp.where(row < s_len - s, pltpu.roll(y, s_len - s, axis=0), 0.0)

    back = down(x, 1)
    fwd = x
    sums = [back + fwd]
    for a in POOL_HALF_WINDOWS[:-1]:
        back = back + down(back, a)
        fwd = fwd + up(fwd, a)
        sums.append(back + fwd)
    grp = lane >> 6
    tot = jnp.where(grp == 0, sums[0], jnp.where(grp == 1, sums[1], jnp.where(grp == 2, sums[2], sums[3])))
    half = jnp.where(grp == 0, POOL_HALF_WINDOWS[0],
                     jnp.where(grp == 1, POOL_HALF_WINDOWS[1],
                               jnp.where(grp == 2, POOL_HALF_WINDOWS[2], POOL_HALF_WINDOWS[3])))
    cnt = (jnp.minimum(row + half, s_len) - jnp.maximum(row - half, 0)).astype(F32)
    d = (tot / cnt - x).astype(BF16)
    return _dot(d, wp) * scale


def _dup_halves(t):
    lane = lax.broadcasted_iota(jnp.int32, t.shape, 1)
    lo = lane < HEAD_DIM
    sw = pltpu.roll(t, HEAD_DIM, axis=1)
    return (jnp.where(lo, t, sw).astype(BF16), jnp.where(lo, sw, t).astype(BF16))


def _core_kernel(*refs, latent, l):
    if latent:
        (x_ref, mod_ref, nw_ref, q_ref, k_ref, v_ref, a_ref, p_ref, ck_ref, cv_ref,
         wo_ref, wp_ref, ps_ref, sink_ref, o_ref, pool_ref, cat_ref) = refs
    else:
        (x_ref, mod_ref, nw_ref, q_ref, k_ref, v_ref, a_ref, p_ref,
         wo_ref, wp_ref, ps_ref, sink_ref, o_ref, pool_ref, cat_ref) = refs
    jq = pl.program_id(1)
    s_len = k_ref.shape[0]

    @pl.when(jq == 0)
    def _():
        pool_ref[...] = _pool(p_ref[...], wp_ref[...], ps_ref[...]).astype(BF16)

    if latent:
        start = jnp.clip(jq * Q_BLK - WINDOW, 0, s_len - BAND)
        start = pl.multiple_of(start, WINDOW)
        kk = _dup_halves(k_ref[pl.ds(start, BAND), :])
        vv = _dup_halves(v_ref[pl.ds(start, BAND), :])
        ckk = _dup_halves(ck_ref[...])
        cvv = _dup_halves(cv_ref[...])
        qpos = jq * Q_BLK + lax.broadcasted_iota(jnp.int32, (Q_BLK, BAND), 0)
        kpos = start + lax.broadcasted_iota(jnp.int32, (Q_BLK, BAND), 1)
        valid = jnp.abs(qpos - kpos) <= WINDOW
        segs = lambda g: [(kk[g], vv[g], valid), (ckk[g], cvv[g], None)]
    else:
        kk = _dup_halves(k_ref[...])
        vv = _dup_halves(v_ref[...])
        segs = lambda g: [(kk[g], vv[g], None)]

    lane = lax.broadcasted_iota(jnp.int32, (Q_BLK, LANES), 1)
    lo = lane < HEAD_DIM
    for g in range(N_KV_HEADS):
        for p in range(2):
            col = (2 * g + p) * LANES
            qpair = q_ref[:, col:col + LANES]
            out_pair = None
            for hf in range(2):
                sel = lo if hf == 0 else jnp.logical_not(lo)
                sink = sink_ref[l, g * 4 + p * 2 + hf]
                qm = jnp.where(sel, qpair, jnp.zeros_like(qpair))
                scores = []
                m = jnp.full((Q_BLK, 1), sink, F32)
                for keys, _, mask in segs(g):
                    s = _dot_t(qm, keys)
                    if mask is not None:
                        s = jnp.where(mask, s, NEG_INF)
                    scores.append(s)
                    m = jnp.maximum(m, jnp.max(s, axis=-1, keepdims=True))
                den = jnp.exp(sink - m)
                o = None
                for s, (_, vals, _) in zip(scores, segs(g)):
                    e = jnp.exp(s - m)
                    den = den + jnp.sum(e, axis=-1, keepdims=True)
                    part = _dot(e.astype(BF16), vals)
                    o = part if o is None else o + part
                o = o / den
                out_pair = o if out_pair is None else jnp.where(sel, o, out_pair)
            cat_ref[:, col:col + LANES] = out_pair.astype(BF16)

    cat_ref[:, ATTN_WIDTH:ATTN_WIDTH + GM_WIDTH] = a_ref[...]
    row0 = pl.multiple_of(jq * Q_BLK, Q_BLK)
    cat_ref[:, ATTN_WIDTH + GM_WIDTH:] = pool_ref[pl.ds(row0, Q_BLK), :]
    out = _dot(cat_ref[...], wo_ref[...])
    gate = mod_ref[0, 5:6, :]
    o_ref[...] = x_ref[...] + gate * _rms(out, nw_ref[3:4, :])


def _mixer_core(x, mod_all, norm_w, q, k, v, a, p, w_out, wp_bd, pool_scale, sink, l,
                cache=None, prev=None):
    latent = cache is not None
    if latent:
        n_seq, s_len, row_off, cond0 = N_LAT_SEQ, LAT_SEQ, T_CTX // Q_BLK, 1
    else:
        n_seq, s_len, row_off, cond0 = N_CTX_SEQ, CTX_SEQ, 0, 0
    nq = s_len // Q_BLK
    seq_off = row_off * Q_BLK // s_len

    blk = lambda b, j: (row_off + b * nq + j, 0)
    seq = lambda b, j: (seq_off + b, 0)
    cond = (lambda b, j: (l, cond0 + b, 0, 0)) if latent else (lambda b, j: (l, 0, 0, 0))
    in_specs = [
        pl.BlockSpec((Q_BLK, D_MODEL), blk),
        pl.BlockSpec((None, 1, N_MOD, D_MODEL), cond),
        pl.BlockSpec((None, 6, D_MODEL), lambda b, j: (l, 0, 0)),
        pl.BlockSpec((Q_BLK, ATTN_WIDTH), blk),
        pl.BlockSpec((s_len, KV_WIDTH), seq),
        pl.BlockSpec((s_len, KV_WIDTH), seq),
        pl.BlockSpec((Q_BLK, GM_WIDTH), blk),
        pl.BlockSpec((s_len, POOL_WIDTH), seq),
    ]
    args = [x, mod_all, norm_w, q, k, v, a, p]
    if latent:
        ck, cv = cache
        in_specs += [
            pl.BlockSpec((None, None, PAST_LEN, KV_WIDTH), lambda b, j: (b, l, 0, 0)),
            pl.BlockSpec((None, None, PAST_LEN, KV_WIDTH), lambda b, j: (b, l, 0, 0)),
        ]
        args += [ck, cv]
    in_specs += [
        pl.BlockSpec((None, D_MODEL, D_MODEL), lambda b, j: (l, 0, 0)),
        pl.BlockSpec((None, POOL_WIDTH, POOL_WIDTH), lambda b, j: (l, 0, 0)),
        pl.BlockSpec((None, 1, POOL_WIDTH), lambda b, j: (l, 0, 0)),
        pl.BlockSpec(memory_space=pltpu.SMEM),
    ]
    args += [w_out, wp_bd, pool_scale, sink]
    aliases = {}
    if prev is not None:
        in_specs.append(pl.BlockSpec(memory_space=pl.ANY))
        args.append(prev)
        aliases = {len(args) - 1: 0}

    kern = functools.partial(_core_kernel, latent=latent, l=l)
    if prev is not None:
        def kern(*refs, _k=kern):
            n_in = len(args)
            return _k(*refs[:n_in - 1], *refs[n_in:])

    return pl.pallas_call(
        kern,
        out_shape=jax.ShapeDtypeStruct((T_ALL, D_MODEL), F32),
        grid=(n_seq, nq),
        in_specs=in_specs,
        out_specs=pl.BlockSpec((Q_BLK, D_MODEL), blk),
        scratch_shapes=[
            pltpu.VMEM((s_len, POOL_WIDTH), BF16),
            pltpu.VMEM((Q_BLK, D_MODEL), BF16),
        ],
        input_output_aliases=aliases,
        compiler_params=pltpu.CompilerParams(
            dimension_semantics=("arbitrary", "arbitrary"), vmem_limit_bytes=VMEM_LIMIT),
        name="mixer_core_lat" if latent else "mixer_core_ctx",
    )(*args)


def _rope_tables():
    t = jnp.arange(LAT_SEQ, dtype=jnp.int32)
    row = (t // GRID_W).astype(F32)
    col = (t % GRID_W).astype(F32)
    nf = HEAD_DIM // 4
    inv = ROPE_BASE ** (-jnp.arange(nf, dtype=F32) / nf)
    ar = row[:, None] * inv
    ac = col[:, None] * inv
    cos64 = jnp.concatenate([jnp.cos(ar), jnp.cos(ar), jnp.cos(ac), jnp.cos(ac)], axis=-1)
    sin64 = jnp.concatenate([-jnp.sin(ar), jnp.sin(ar), -jnp.sin(ac), jnp.sin(ac)], axis=-1)
    return jnp.tile(cos64, (1, 2)), jnp.tile(sin64, (1, 2))


def kernel(x_prompt, x_sample, cache_k, cache_v, c, c_ctx, w_mod, b_mod, norm_w, w_in, w_out,
           attn_sink, w_spatial, b_spatial, w_pool, pool_scale, ffn_w1, ffn_w2):
    cond = jnp.concatenate(
        [c_ctx[None, :], c, jnp.zeros((N_COND - 1 - N_LAT_SEQ, D_MODEL), F32)], axis=0)
    mod_all = _adaln(cond, w_mod, b_mod).reshape(DEPTH, N_COND, N_MOD, D_MODEL)

    w1 = ffn_w1.astype(BF16)
    w2 = ffn_w2.astype(BF16)
    w_in_b = w_in.astype(BF16)
    w_out_b = w_out.astype(BF16)
    ws_cat = w_spatial.transpose(0, 2, 1, 3).reshape(DEPTH, CHUNK, GM_HEADS * CHUNK).astype(BF16)
    bz = jnp.repeat(b_spatial.transpose(0, 2, 1), GM_DIM, axis=2)
    eye = jnp.eye(len(POOL_HALF_WINDOWS), dtype=F32)
    wp_bd = jnp.einsum('lgij,gh->lgihj', w_pool, eye).reshape(DEPTH, POOL_WIDTH, POOL_WIDTH).astype(BF16)
    ps = pool_scale.reshape(DEPTH, 1, POOL_WIDTH)
    cos_t, sin_t = _rope_tables()
    ck = cache_k.reshape(N_LAT_SEQ, DEPTH, PAST_LEN, KV_WIDTH)
    cv = cache_v.reshape(N_LAT_SEQ, DEPTH, PAST_LEN, KV_WIDTH)

    x = jnp.concatenate([x_prompt.reshape(T_CTX, D_MODEL), x_sample.reshape(T_LAT, D_MODEL)], axis=0)
    new_k, new_v = [], []
    for l in range(DEPTH):
        x = _ffn_half(x, mod_all, norm_w, w1, w2, l, 0)
        q, k, v, a, p = _mixer_proj(x, mod_all, norm_w, w_in_b, cos_t, sin_t, ws_cat, bz, l)
        y = _mixer_core(x, mod_all, norm_w, q, k, v, a, p, w_out_b, wp_bd, ps, attn_sink, l)
        x = _mixer_core(x, mod_all, norm_w, q, k, v, a, p, w_out_b, wp_bd, ps, attn_sink, l,
                        cache=(ck, cv), prev=y)
        x = _ffn_half(x, mod_all, norm_w, w1, w2, l, 1)
        new_k.append(k[:T_CTX].reshape(N_CTX_SEQ, CTX_SEQ, N_KV_HEADS, HEAD_DIM))
        new_v.append(v[:T_CTX].reshape(N_CTX_SEQ, CTX_SEQ, N_KV_HEADS, HEAD_DIM))
    y_prompt = x[:T_CTX].reshape(N_CTX_SEQ, CTX_SEQ, D_MODEL)
    y_sample = x[T_CTX:].reshape(N_LAT_SEQ, LAT_SEQ, D_MODEL)
    return y_prompt, y_sample, jnp.stack(new_k, axis=1), jnp.stack(new_v, axis=1)
```

```python
import functools

import jax
import jax.numpy as jnp
from jax import lax
from jax.experimental import pallas as pl
from jax.experimental.pallas import tpu as pltpu

D_MODEL = 1024
N_CTX_SEQ = 32
CTX_SEQ = 256
DEPTH = 4
N_LAT_SEQ = 2
LAT_SEQ = 2048
PAST_LEN = 512
GRID_W = 64
HEAD_DIM = 64
ATTN_WIDTH = 512
N_HEADS = 8
N_KV_HEADS = 2
KV_WIDTH = 128
WINDOW = 128
GM_WIDTH = 256
GM_HEADS = 4
GM_DIM = 64
CHUNK = 128
POOL_WIDTH = 256
POOL_HALF_WINDOWS = (1, 2, 4, 8)
POOL_DIM = 64
IN_WIDTH = 1536
D_FF = 2816
N_MOD = 9
EPS = 1e-6
ROPE_BASE = 10000.0
NEG_INF = -1e30
ATTN_SCALE = HEAD_DIM ** -0.5

T_CTX = N_CTX_SEQ * CTX_SEQ
T_LAT = N_LAT_SEQ * LAT_SEQ
T_ALL = T_CTX + T_LAT
N_COND = 8

LANES = 128
VMEM_LIMIT = 56 * 1024 * 1024

FFN_TM = 1024
FFN_CHUNKS = (512, 512, 512, 512, 512, 256)
PROJ_TM = 512
Q_BLK = 256
BAND = Q_BLK + 2 * WINDOW
MOD_TN = 2304

BF16 = jnp.bfloat16
F32 = jnp.float32


def _dot(a, b):
    return jnp.dot(a, b, preferred_element_type=F32)


def _dot_t(a, b):
    return lax.dot_general(a, b, (((1,), (1,)), ((), ())), preferred_element_type=F32)


def _rms(x, g):
    return x * lax.rsqrt(jnp.mean(x * x, axis=-1, keepdims=True) + EPS) * g


def _cond_row(tile, tm):
    n_ctx = T_CTX // tm
    per_seq = LAT_SEQ // tm
    return jnp.where(tile < n_ctx, 0, 1 + (tile - n_ctx) // per_seq)


def _mod_kernel(c_ref, w_ref, b_ref, o_ref):
    c = c_ref[...]
    s = jax.nn.silu(c).astype(BF16)
    o_ref[...] = _dot(s, w_ref[...].astype(BF16)) + b_ref[...]


def _adaln(cond, w_mod, b_mod):
    n_out = N_MOD * D_MODEL
    return pl.pallas_call(
        _mod_kernel,
        out_shape=jax.ShapeDtypeStruct((DEPTH, N_COND, n_out), F32),
        grid=(DEPTH, n_out // MOD_TN),
        in_specs=[
            pl.BlockSpec((N_COND, D_MODEL), lambda l, n: (0, 0)),
            pl.BlockSpec((None, D_MODEL, MOD_TN), lambda l, n: (l, 0, n)),
            pl.BlockSpec((None, 1, MOD_TN), lambda l, n: (l, 0, n)),
        ],
        out_specs=pl.BlockSpec((None, N_COND, MOD_TN), lambda l, n: (l, 0, n)),
        compiler_params=pltpu.CompilerParams(
            dimension_semantics=("arbitrary", "arbitrary"), vmem_limit_bytes=VMEM_LIMIT),
        name="adaln",
    )(cond, w_mod, b_mod.reshape(DEPTH, 1, n_out))


def _ffn_kernel(*refs, j, pre, post, split_in, split_out):
    n_x = 2 if split_in else 1
    x_refs = refs[:n_x]
    mod_ref, nw_ref, w1_ref, w2_ref = refs[n_x:n_x + 4]
    o_refs = refs[n_x + 4:]
    is_ctx = pl.program_id(0) < T_CTX // FFN_TM
    if split_in:
        x = jnp.where(is_ctx, x_refs[0][...], x_refs[1][...])
    else:
        x = x_refs[0][...]
    shift = mod_ref[0, j:j + 1, :]
    scale = mod_ref[0, j + 1:j + 2, :]
    gate = mod_ref[0, j + 2:j + 3, :]
    h = (_rms(x, nw_ref[pre:pre + 1, :]) * (1 + scale) + shift).astype(BF16)
    acc = None
    off = 0
    for tk in FFN_CHUNKS:
        g = _dot(h, w1_ref[:, off:off + tk])
        u = _dot(h, w1_ref[:, D_FF + off:D_FF + off + tk])
        a = (jax.nn.silu(g) * u).astype(BF16)
        part = _dot(a, w2_ref[off:off + tk, :])
        acc = part if acc is None else acc + part
        off += tk
    y = x + 0.5 * gate * _rms(acc, nw_ref[post:post + 1, :])
    if split_out:
        @pl.when(is_ctx)
        def _():
            o_refs[0][...] = y

        @pl.when(jnp.logical_not(is_ctx))
        def _():
            o_refs[1][...] = y
    else:
        o_refs[0][...] = y


def _ffn_half(xs, mod_all, norm_w, w1, w2, l, half, split_out=False):
    j, pre, post = (0, 0, 1) if half == 0 else (6, 4, 5)
    tm = FFN_TM
    n_ctx = T_CTX // tm
    split_in = len(xs) == 2
    row = lambda i: (i, 0)
    ctx_row = lambda i: (jnp.minimum(i, n_ctx - 1), 0)
    lat_row = lambda i: (jnp.maximum(i - n_ctx, 0), 0)
    tile = (tm, D_MODEL)
    if split_in:
        x_specs = [pl.BlockSpec(tile, ctx_row), pl.BlockSpec(tile, lat_row)]
    else:
        x_specs = [pl.BlockSpec(tile, row)]
    if split_out:
        out_shape = (jax.ShapeDtypeStruct((T_CTX, D_MODEL), F32),
                     jax.ShapeDtypeStruct((T_LAT, D_MODEL), F32))
        out_specs = (pl.BlockSpec(tile, ctx_row), pl.BlockSpec(tile, lat_row))
    else:
        out_shape = jax.ShapeDtypeStruct((T_ALL, D_MODEL), F32)
        out_specs = pl.BlockSpec(tile, row)
    return pl.pallas_call(
        functools.partial(_ffn_kernel, j=j, pre=pre, post=post,
                          split_in=split_in, split_out=split_out),
        out_shape=out_shape,
        grid=(T_ALL // tm,),
        in_specs=x_specs + [
            pl.BlockSpec((None, 1, N_MOD, D_MODEL), lambda i: (l, _cond_row(i, tm), 0, 0)),
            pl.BlockSpec((None, 6, D_MODEL), lambda i: (l, 0, 0)),
            pl.BlockSpec((None, None, D_MODEL, 2 * D_FF), lambda i: (l, half, 0, 0),
                         pipeline_mode=pl.Buffered(1)),
            pl.BlockSpec((None, None, D_FF, D_MODEL), lambda i: (l, half, 0, 0),
                         pipeline_mode=pl.Buffered(1)),
        ],
        out_specs=out_specs,
        compiler_params=pltpu.CompilerParams(
            dimension_semantics=("arbitrary",), vmem_limit_bytes=VMEM_LIMIT),
        name="ffn_half",
    )(*xs, mod_all, norm_w, w1, w2)


def _swap16(x):
    w = x.shape[-1]
    lane = lax.broadcasted_iota(jnp.int32, x.shape, x.ndim - 1)
    first = (lane & 31) < 16
    return jnp.where(first, pltpu.roll(x, w - 16, axis=x.ndim - 1), pltpu.roll(x, 16, axis=x.ndim - 1))


def _rope(x, cos_t, sin_t):
    reps = x.shape[-1] // LANES
    c = jnp.concatenate([cos_t] * reps, axis=-1) if reps > 1 else cos_t
    s = jnp.concatenate([sin_t] * reps, axis=-1) if reps > 1 else sin_t
    return x * c + _swap16(x) * s


def _proj_kernel(x_ref, mod_ref, nw_ref, w_ref, cos_ref, sin_ref, ws_ref, bz_ref, *rest):
    q_ref, kc_ref, vc_ref, kl_ref, vl_ref, a_ref, p_ref = rest[-7:]
    i = pl.program_id(0)
    tm = x_ref.shape[0]
    x = x_ref[...]
    shift = mod_ref[0, 3:4, :]
    scale = mod_ref[0, 4:5, :]
    h = (_rms(x, nw_ref[2:3, :]) * (1 + scale) + shift).astype(BF16)
    proj = _dot(h, w_ref[...])
    q = proj[:, 0:ATTN_WIDTH] * ATTN_SCALE
    k = proj[:, ATTN_WIDTH:ATTN_WIDTH + KV_WIDTH]
    v = proj[:, ATTN_WIDTH + KV_WIDTH:ATTN_WIDTH + 2 * KV_WIDTH]
    gu = proj[:, 768:1024]
    gv = proj[:, 1024:1280]
    p_ref[...] = proj[:, 1280:1536]

    is_lat = i >= T_CTX // tm

    @pl.when(is_lat)
    def _():
        cos_t = cos_ref[...]
        sin_t = sin_ref[...]
        q_ref[...] = _rope(q, cos_t, sin_t).astype(BF16)
        kl_ref[...] = _rope(k, cos_t, sin_t)
        vl_ref[...] = v

    @pl.when(jnp.logical_not(is_lat))
    def _():
        q_ref[...] = q.astype(BF16)
        for s in range(tm // CTX_SEQ):
            kc_ref[s] = k[s * CTX_SEQ:(s + 1) * CTX_SEQ, :]
            vc_ref[s] = v[s * CTX_SEQ:(s + 1) * CTX_SEQ, :]

    r = lax.broadcasted_iota(jnp.int32, (GM_WIDTH, GM_WIDTH), 0)
    c = lax.broadcasted_iota(jnp.int32, (GM_WIDTH, GM_WIDTH), 1)
    grp_mean = jnp.where((r >> 6) == (c >> 6), 1.0 / GM_DIM, 0.0).astype(BF16)
    sq = gv * gv
    sq_hi = sq.astype(BF16)
    sq_lo = (sq - sq_hi.astype(F32)).astype(BF16)
    ms = _dot(sq_hi, grp_mean) + _dot(sq_lo, grp_mean)
    vh = (gv * lax.rsqrt(ms + EPS)).astype(BF16)
    lane = lax.broadcasted_iota(jnp.int32, (CHUNK, GM_WIDTH), 1)
    ws = ws_ref[...]
    bz = bz_ref[...]
    for n in range(tm // CHUNK):
        vn = vh[n * CHUNK:(n + 1) * CHUNK, :]
        bd = jnp.concatenate(
            [jnp.where((lane >> 6) == hh, vn, jnp.zeros_like(vn)) for hh in range(GM_HEADS)], axis=0)
        z = _dot(ws, bd) + bz
        a_ref[n * CHUNK:(n + 1) * CHUNK, :] = (gu[n * CHUNK:(n + 1) * CHUNK, :] * z).astype(BF16)


def _mixer_proj(x, mod_all, norm_w, w_in, cos_t, sin_t, ws_cat, bz, l, caches=None):
    tm = PROJ_TM
    n_ctx = T_CTX // tm
    per_seq = LAT_SEQ // tm
    seq_per_tile = tm // CTX_SEQ

    def tbl_map(i):
        return (lax.rem(jnp.maximum(i - n_ctx, 0), per_seq), 0)

    row = lambda i: (i, 0)
    lat_row = lambda i: (jnp.maximum(i - n_ctx, 0), 0)
    cache_blk = pl.BlockSpec((seq_per_tile, None, CTX_SEQ, KV_WIDTH),
                             lambda i: (jnp.minimum(i, n_ctx - 1), l, 0, 0))
    cache_shape = jax.ShapeDtypeStruct((N_CTX_SEQ, DEPTH, CTX_SEQ, KV_WIDTH), F32)
    extra_specs, extra_args, aliases = [], [], {}
    if caches is not None:
        extra_specs = [pl.BlockSpec(memory_space=pl.ANY)] * 2
        extra_args = list(caches)
        aliases = {8: 1, 9: 2}
    return pl.pallas_call(
        _proj_kernel,
        out_shape=(
            jax.ShapeDtypeStruct((T_ALL, ATTN_WIDTH), BF16),
            cache_shape,
            cache_shape,
            jax.ShapeDtypeStruct((T_LAT, KV_WIDTH), F32),
            jax.ShapeDtypeStruct((T_LAT, KV_WIDTH), F32),
            jax.ShapeDtypeStruct((T_ALL, GM_WIDTH), BF16),
            jax.ShapeDtypeStruct((T_ALL, POOL_WIDTH), F32),
        ),
        grid=(T_ALL // tm,),
        input_output_aliases=aliases,
        in_specs=[
            pl.BlockSpec((tm, D_MODEL), row),
            pl.BlockSpec((None, 1, N_MOD, D_MODEL), lambda i: (l, _cond_row(i, tm), 0, 0)),
            pl.BlockSpec((None, 6, D_MODEL), lambda i: (l, 0, 0)),
            pl.BlockSpec((None, D_MODEL, IN_WIDTH), lambda i: (l, 0, 0)),
            pl.BlockSpec((tm, LANES), tbl_map),
            pl.BlockSpec((tm, LANES), tbl_map),
            pl.BlockSpec((None, CHUNK, GM_HEADS * CHUNK), lambda i: (l, 0, 0)),
            pl.BlockSpec((None, CHUNK, GM_WIDTH), lambda i: (l, 0, 0)),
        ] + extra_specs,
        out_specs=(
            pl.BlockSpec((tm, ATTN_WIDTH), row),
            cache_blk,
            cache_blk,
            pl.BlockSpec((tm, KV_WIDTH), lat_row),
            pl.BlockSpec((tm, KV_WIDTH), lat_row),
            pl.BlockSpec((tm, GM_WIDTH), row),
            pl.BlockSpec((tm, POOL_WIDTH), row),
        ),
        compiler_params=pltpu.CompilerParams(
            dimension_semantics=("arbitrary",), vmem_limit_bytes=VMEM_LIMIT),
        name="mixer_proj",
    )(x, mod_all, norm_w, w_in, cos_t, sin_t, ws_cat, bz, *extra_args)


def _pool(x, wp, scale):
    s_len = x.shape[0]
    row = lax.broadcasted_iota(jnp.int32, x.shape, 0)
    lane = lax.broadcasted_iota(jnp.int32, x.shape, 1)

    def down(y, s):
        return jnp.where(row >= s, pltpu.roll(y, s, axis=0), 0.0)

    def up(y, s):
        return jnp.where(row < s_len - s, pltpu.roll(y, s_len - s, axis=0), 0.0)

    back = down(x, 1)
    fwd = x
    sums = [back + fwd]
    for a in POOL_HALF_WINDOWS[:-1]:
        back = back + down(back, a)
        fwd = fwd + up(fwd, a)
        sums.append(back + fwd)
    grp = lane >> 6
    tot = jnp.where(grp == 0, sums[0], jnp.where(grp == 1, sums[1], jnp.where(grp == 2, sums[2], sums[3])))
    half = jnp.where(grp == 0, POOL_HALF_WINDOWS[0],
                     jnp.where(grp == 1, POOL_HALF_WINDOWS[1],
                               jnp.where(grp == 2, POOL_HALF_WINDOWS[2], POOL_HALF_WINDOWS[3])))
    cnt = (jnp.minimum(row + half, s_len) - jnp.maximum(row - half, 0)).astype(F32)
    d = (tot / cnt - x).astype(BF16)
    return _dot(d, wp) * scale


def _dup_halves(t):
    lane = lax.broadcasted_iota(jnp.int32, t.shape, 1)
    lo = lane < HEAD_DIM
    sw = pltpu.roll(t, HEAD_DIM, axis=1)
    return (jnp.where(lo, t, sw).astype(BF16), jnp.where(lo, sw, t).astype(BF16))


def _core_kernel(*refs, latent, l):
    if latent:
        (x_ref, mod_ref, nw_ref, q_ref, k_ref, v_ref, a_ref, p_ref, ck_ref, cv_ref,
         wo_ref, wp_ref, ps_ref, sink_ref, o_ref, pool_ref, cat_ref) = refs
    else:
        (x_ref, mod_ref, nw_ref, q_ref, k_ref, v_ref, a_ref, p_ref,
         wo_ref, wp_ref, ps_ref, sink_ref, o_ref, pool_ref, cat_ref) = refs
    jq = pl.program_id(1)
    s_len = k_ref.shape[0]

    @pl.when(jq == 0)
    def _():
        pool_ref[...] = _pool(p_ref[...], wp_ref[...], ps_ref[...]).astype(BF16)

    if latent:
        start = jnp.clip(jq * Q_BLK - WINDOW, 0, s_len - BAND)
        start = pl.multiple_of(start, WINDOW)
        kk = _dup_halves(k_ref[pl.ds(start, BAND), :])
        vv = _dup_halves(v_ref[pl.ds(start, BAND), :])
        ckk = _dup_halves(ck_ref[...])
        cvv = _dup_halves(cv_ref[...])
        qpos = jq * Q_BLK + lax.broadcasted_iota(jnp.int32, (Q_BLK, BAND), 0)
        kpos = start + lax.broadcasted_iota(jnp.int32, (Q_BLK, BAND), 1)
        valid = jnp.abs(qpos - kpos) <= WINDOW
        segs = lambda g: [(kk[g], vv[g], valid), (ckk[g], cvv[g], None)]
    else:
        kk = _dup_halves(k_ref[...])
        vv = _dup_halves(v_ref[...])
        segs = lambda g: [(kk[g], vv[g], None)]

    lane = lax.broadcasted_iota(jnp.int32, (Q_BLK, LANES), 1)
    lo = lane < HEAD_DIM
    for g in range(N_KV_HEADS):
        for p in range(2):
            col = (2 * g + p) * LANES
            qpair = q_ref[:, col:col + LANES]
            out_pair = None
            for hf in range(2):
                sel = lo if hf == 0 else jnp.logical_not(lo)
                sink = sink_ref[l, g * 4 + p * 2 + hf]
                qm = jnp.where(sel, qpair, jnp.zeros_like(qpair))
                scores = []
                m = jnp.full((Q_BLK, 1), sink, F32)
                for keys, _, mask in segs(g):
                    s = _dot_t(qm, keys)
                    if mask is not None:
                        s = jnp.where(mask, s, NEG_INF)
                    scores.append(s)
                    m = jnp.maximum(m, jnp.max(s, axis=-1, keepdims=True))
                den = jnp.exp(sink - m)
                o = None
                for s, (_, vals, _) in zip(scores, segs(g)):
                    e = jnp.exp(s - m)
                    den = den + jnp.sum(e, axis=-1, keepdims=True)
                    part = _dot(e.astype(BF16), vals)
                    o = part if o is None else o + part
                o = o / den
                out_pair = o if out_pair is None else jnp.where(sel, o, out_pair)
            cat_ref[:, col:col + LANES] = out_pair.astype(BF16)

    cat_ref[:, ATTN_WIDTH:ATTN_WIDTH + GM_WIDTH] = a_ref[...]
    row0 = pl.multiple_of(jq * Q_BLK, Q_BLK)
    cat_ref[:, ATTN_WIDTH + GM_WIDTH:] = pool_ref[pl.ds(row0, Q_BLK), :]
    out = _dot(cat_ref[...], wo_ref[...])
    gate = mod_ref[0, 5:6, :]
    o_ref[...] = x_ref[...] + gate * _rms(out, nw_ref[3:4, :])


def _mixer_core(x, mod_all, norm_w, q, k, v, a, p, w_out, wp_bd, pool_scale, sink, l,
                cache=None, prev=None):
    latent = cache is not None
    if latent:
        n_seq, s_len, row_off, cond0 = N_LAT_SEQ, LAT_SEQ, T_CTX // Q_BLK, 1
    else:
        n_seq, s_len, row_off, cond0 = N_CTX_SEQ, CTX_SEQ, 0, 0
    nq = s_len // Q_BLK
    seq_off = row_off * Q_BLK // s_len

    blk = lambda b, j: (row_off + b * nq + j, 0)
    seq = lambda b, j: (seq_off + b, 0)
    cond = (lambda b, j: (l, cond0 + b, 0, 0)) if latent else (lambda b, j: (l, 0, 0, 0))
    if latent:
        kv_spec = pl.BlockSpec((s_len, KV_WIDTH), lambda b, j: (b, 0))
    else:
        kv_spec = pl.BlockSpec((None, None, s_len, KV_WIDTH), lambda b, j: (b, l, 0, 0))
    in_specs = [
        pl.BlockSpec((Q_BLK, D_MODEL), blk),
        pl.BlockSpec((None, 1, N_MOD, D_MODEL), cond),
        pl.BlockSpec((None, 6, D_MODEL), lambda b, j: (l, 0, 0)),
        pl.BlockSpec((Q_BLK, ATTN_WIDTH), blk),
        kv_spec,
        kv_spec,
        pl.BlockSpec((Q_BLK, GM_WIDTH), blk),
        pl.BlockSpec((s_len, POOL_WIDTH), seq),
    ]
    args = [x, mod_all, norm_w, q, k, v, a, p]
    if latent:
        ck, cv = cache
        in_specs += [
            pl.BlockSpec((None, None, PAST_LEN, KV_WIDTH), lambda b, j: (b, l, 0, 0)),
            pl.BlockSpec((None, None, PAST_LEN, KV_WIDTH), lambda b, j: (b, l, 0, 0)),
        ]
        args += [ck, cv]
    in_specs += [
        pl.BlockSpec((None, D_MODEL, D_MODEL), lambda b, j: (l, 0, 0)),
        pl.BlockSpec((None, POOL_WIDTH, POOL_WIDTH), lambda b, j: (l, 0, 0)),
        pl.BlockSpec((None, 1, POOL_WIDTH), lambda b, j: (l, 0, 0)),
        pl.BlockSpec(memory_space=pltpu.SMEM),
    ]
    args += [w_out, wp_bd, pool_scale, sink]
    aliases = {}
    if prev is not None:
        in_specs.append(pl.BlockSpec(memory_space=pl.ANY))
        args.append(prev)
        aliases = {len(args) - 1: 0}

    kern = functools.partial(_core_kernel, latent=latent, l=l)
    if prev is not None:
        def kern(*refs, _k=kern):
            n_in = len(args)
            return _k(*refs[:n_in - 1], *refs[n_in:])

    return pl.pallas_call(
        kern,
        out_shape=jax.ShapeDtypeStruct((T_ALL, D_MODEL), F32),
        grid=(n_seq, nq),
        in_specs=in_specs,
        out_specs=pl.BlockSpec((Q_BLK, D_MODEL), blk),
        scratch_shapes=[
            pltpu.VMEM((s_len, POOL_WIDTH), BF16),
            pltpu.VMEM((Q_BLK, D_MODEL), BF16),
        ],
        input_output_aliases=aliases,
        compiler_params=pltpu.CompilerParams(
            dimension_semantics=("arbitrary", "arbitrary"), vmem_limit_bytes=VMEM_LIMIT),
        name="mixer_core_lat" if latent else "mixer_core_ctx",
    )(*args)


def _rope_tables():
    t = jnp.arange(LAT_SEQ, dtype=jnp.int32)
    row = (t // GRID_W).astype(F32)
    col = (t % GRID_W).astype(F32)
    nf = HEAD_DIM // 4
    inv = ROPE_BASE ** (-jnp.arange(nf, dtype=F32) / nf)
    ar = row[:, None] * inv
    ac = col[:, None] * inv
    cos64 = jnp.concatenate([jnp.cos(ar), jnp.cos(ar), jnp.cos(ac), jnp.cos(ac)], axis=-1)
    sin64 = jnp.concatenate([-jnp.sin(ar), jnp.sin(ar), -jnp.sin(ac), jnp.sin(ac)], axis=-1)
    return jnp.tile(cos64, (1, 2)), jnp.tile(sin64, (1, 2))


def kernel(x_prompt, x_sample, cache_k, cache_v, c, c_ctx, w_mod, b_mod, norm_w, w_in, w_out,
           attn_sink, w_spatial, b_spatial, w_pool, pool_scale, ffn_w1, ffn_w2):
    cond = jnp.concatenate(
        [c_ctx[None, :], c, jnp.zeros((N_COND - 1 - N_LAT_SEQ, D_MODEL), F32)], axis=0)
    mod_all = _adaln(cond, w_mod, b_mod).reshape(DEPTH, N_COND, N_MOD, D_MODEL)

    w1 = ffn_w1.astype(BF16)
    w2 = ffn_w2.astype(BF16)
    w_in_b = w_in.astype(BF16)
    w_out_b = w_out.astype(BF16)
    ws_cat = w_spatial.transpose(0, 2, 1, 3).reshape(DEPTH, CHUNK, GM_HEADS * CHUNK).astype(BF16)
    bz = jnp.repeat(b_spatial.transpose(0, 2, 1), GM_DIM, axis=2)
    eye = jnp.eye(len(POOL_HALF_WINDOWS), dtype=F32)
    wp_bd = jnp.einsum('lgij,gh->lgihj', w_pool, eye).reshape(DEPTH, POOL_WIDTH, POOL_WIDTH).astype(BF16)
    ps = pool_scale.reshape(DEPTH, 1, POOL_WIDTH)
    cos_t, sin_t = _rope_tables()
    ck = cache_k.reshape(N_LAT_SEQ, DEPTH, PAST_LEN, KV_WIDTH)
    cv = cache_v.reshape(N_LAT_SEQ, DEPTH, PAST_LEN, KV_WIDTH)

    xs = (x_prompt.reshape(T_CTX, D_MODEL), x_sample.reshape(T_LAT, D_MODEL))
    caches = None
    for l in range(DEPTH):
        x = _ffn_half(xs, mod_all, norm_w, w1, w2, l, 0)
        q, kc, vc, kl, vl, a, p = _mixer_proj(
            x, mod_all, norm_w, w_in_b, cos_t, sin_t, ws_cat, bz, l, caches=caches)
        caches = (kc, vc)
        y = _mixer_core(x, mod_all, norm_w, q, kc, vc, a, p, w_out_b, wp_bd, ps, attn_sink, l)
        x = _mixer_core(x, mod_all, norm_w, q, kl, vl, a, p, w_out_b, wp_bd, ps, attn_sink, l,
                        cache=(ck, cv), prev=y)
        xs = _ffn_half((x,), mod_all, norm_w, w1, w2, l, 1, split_out=(l == DEPTH - 1))
        if l < DEPTH - 1:
            xs = (xs,)
    y_prompt = xs[0].reshape(N_CTX_SEQ, CTX_SEQ, D_MODEL)
    y_sample = xs[1].reshape(N_LAT_SEQ, LAT_SEQ, D_MODEL)
    cache_shape = (N_CTX_SEQ, DEPTH, CTX_SEQ, N_KV_HEADS, HEAD_DIM)
    return y_prompt, y_sample, caches[0].reshape(cache_shape), caches[1].reshape(cache_shape)
```

```python
import functools

import jax
import jax.numpy as jnp
from jax import lax
from jax.experimental import pallas as pl
from jax.experimental.pallas import tpu as pltpu

D_MODEL = 1024
N_CTX_SEQ = 32
CTX_SEQ = 256
DEPTH = 4
N_LAT_SEQ = 2
LAT_SEQ = 2048
PAST_LEN = 512
GRID_W = 64
HEAD_DIM = 64
ATTN_WIDTH = 512
N_HEADS = 8
N_KV_HEADS = 2
KV_WIDTH = 128
WINDOW = 128
GM_WIDTH = 256
GM_HEADS = 4
GM_DIM = 64
CHUNK = 128
POOL_WIDTH = 256
POOL_HALF_WINDOWS = (1, 2, 4, 8)
POOL_DIM = 64
IN_WIDTH = 1536
D_FF = 2816
N_MOD = 9
EPS = 1e-6
ROPE_BASE = 10000.0
NEG_INF = -1e30
ATTN_SCALE = HEAD_DIM ** -0.5

T_CTX = N_CTX_SEQ * CTX_SEQ
T_LAT = N_LAT_SEQ * LAT_SEQ
T_ALL = T_CTX + T_LAT
N_COND = 8

LANES = 128
VMEM_LIMIT = 56 * 1024 * 1024

FFN_TM = 512
FFN_CHUNKS = (512, 512, 512, 512, 512, 256)
W1_SIDE_BLOCKS = 16
W2_SIDE_BLOCKS = 22
WIO_SIDE_BLOCKS = 16
PROJ_TM = 512
Q_BLK = 256
BAND = Q_BLK + 2 * WINDOW
MOD_TN = 2304

BF16 = jnp.bfloat16
F32 = jnp.float32


def _dot(a, b):
    return jnp.dot(a, b, preferred_element_type=F32)


def _dot_t(a, b):
    return lax.dot_general(a, b, (((1,), (1,)), ((), ())), preferred_element_type=F32)


def _rms(x, g):
    return x * lax.rsqrt(jnp.mean(x * x, axis=-1, keepdims=True) + EPS) * g


def _zero_from(v):
    bits = lax.bitcast_convert_type(v, jnp.uint32)
    return ((bits >> 16) >> 16).astype(F32)


def _cond_row(tile, tm):
    n_ctx = T_CTX // tm
    per_seq = LAT_SEQ // tm
    return jnp.where(tile < n_ctx, 0, 1 + (tile - n_ctx) // per_seq)


def _mod_kernel(c_ref, w_ref, b_ref, o_ref):
    c = c_ref[...]
    s = jax.nn.silu(c).astype(BF16)
    o_ref[...] = _dot(s, w_ref[...].astype(BF16)) + b_ref[...]


def _adaln(cond, w_mod, b_mod):
    n_out = N_MOD * D_MODEL
    return pl.pallas_call(
        _mod_kernel,
        out_shape=jax.ShapeDtypeStruct((DEPTH, N_COND, n_out), F32),
        grid=(DEPTH, n_out // MOD_TN),
        in_specs=[
            pl.BlockSpec((N_COND, D_MODEL), lambda l, n: (0, 0)),
            pl.BlockSpec((None, D_MODEL, MOD_TN), lambda l, n: (l, 0, n)),
            pl.BlockSpec((None, 1, MOD_TN), lambda l, n: (l, 0, n)),
        ],
        out_specs=pl.BlockSpec((None, N_COND, MOD_TN), lambda l, n: (l, 0, n)),
        compiler_params=pltpu.CompilerParams(
            dimension_semantics=("arbitrary", "arbitrary"), vmem_limit_bytes=VMEM_LIMIT),
        name="adaln",
    )(cond, w_mod, b_mod.reshape(DEPTH, 1, n_out))


def _ffn_kernel(*refs, j, pre, post, split_in, split_out, n_side):
    n_x = 2 if split_in else 1
    n_o = 2 if split_out else 1
    xp_refs = refs[:n_x]
    xn_refs = refs[n_x:2 * n_x]
    modp_ref, modn_ref, nw_ref, w1_ref, w2_ref = refs[2 * n_x:2 * n_x + 5]
    pos = 2 * n_x + 5
    side_in = refs[pos:pos + n_side]
    o_refs = refs[pos + n_side:pos + n_side + n_o]
    side_out = refs[pos + n_side + n_o:pos + 2 * n_side + n_o]
    scratch = refs[pos + 2 * n_side + n_o:]
    h_refs, acc_ref = scratch[:2], scratch[2]
    i = pl.program_id(0)
    n_tiles = T_ALL // FFN_TM
    n_ctx = T_CTX // FFN_TM

    def load_x(x_refs, tile, rows):
        if split_in:
            return jnp.where(tile < n_ctx, x_refs[0][rows, :], x_refs[1][rows, :])
        return x_refs[0][rows, :]

    def prologue(x_refs, mod_ref, tile, rows):
        shift = mod_ref[0, j:j + 1, :]
        scale = mod_ref[0, j + 1:j + 2, :]
        x = load_x(x_refs, tile, rows)
        return _rms(x, nw_ref[pre:pre + 1, :]) * (1 + scale) + shift

    def epilogue(rows):
        gate = modp_ref[0, j + 2:j + 3, :]
        y = (load_x(xp_refs, i - 1, rows)
             + 0.5 * gate * _rms(acc_ref[rows, :], nw_ref[post:post + 1, :]))
        if split_out:
            scratch[3][rows, :] = y
        else:
            o_refs[0][rows, :] = y
        return y

    def store_split():
        prev_ctx = i - 1 < n_ctx

        @pl.when(prev_ctx)
        def _():
            o_refs[0][...] = scratch[3][...]

        @pl.when(jnp.logical_not(prev_ctx))
        def _():
            o_refs[1][...] = scratch[3][...]

    all_rows = slice(0, FFN_TM)
    halves = (slice(0, FFN_TM // 2), slice(FFN_TM // 2, FFN_TM))

    @pl.when(i == 0)
    def _():
        h_refs[0][...] = prologue(xp_refs, modp_ref, 0, all_rows).astype(BF16)
        acc_ref[...] = jnp.zeros_like(acc_ref)

    def main(h_cur, h_nxt):
        def vpu_piece(c):
            if c in (0, 1):
                return _zero_from(epilogue(halves[c])[:, 0:1])
            if c in (2, 3):
                rows = halves[c - 2]
                hf = prologue(xn_refs, modn_ref, i + 1, rows)
                h_nxt[rows, :] = hf.astype(BF16)
                return _zero_from(hf[:, 0:1])
            if c == 4:
                for s_in, s_out in zip(side_in, side_out):
                    s_out[...] = s_in[...].astype(BF16)
            return None

        acc = None
        off = 0
        tie = None
        for c, tk in enumerate(FFN_CHUNKS):
            g = _dot(h_cur[...], w1_ref[:, off:off + tk])
            u = _dot(h_cur[...], w1_ref[:, D_FF + off:D_FF + off + tk])
            if tie is not None:
                tied = halves[(c - 1) % 2]
                u = jnp.concatenate(
                    [u[r, :] + tie if r is tied else u[r, :] for r in halves], axis=0)
            a = (jax.nn.silu(g) * u).astype(BF16)
            part = _dot(a, w2_ref[off:off + tk, :])
            acc = part if acc is None else acc + part
            off += tk
            tie = vpu_piece(c)
        acc_ref[...] = acc

    for par in range(2):
        @pl.when(jnp.logical_and(i < n_tiles, lax.rem(i, 2) == par))
        def _(par=par):
            main(h_refs[par], h_refs[1 - par])

    @pl.when(i == n_tiles)
    def _():
        epilogue(all_rows)

    if split_out:
        store_split()


def _side_job(arr, lead, n_blk):
    rows, cols = arr.shape[-2:]
    blk = rows // n_blk
    nones = (None,) * len(lead)
    in_spec = pl.BlockSpec(nones + (blk, cols), lambda i: lead + (jnp.minimum(i, n_blk - 1), 0))
    out_spec = pl.BlockSpec((blk, cols), lambda i: (jnp.minimum(i, n_blk - 1), 0))
    return arr, in_spec, out_spec, jax.ShapeDtypeStruct((rows, cols), BF16)


def _ffn_half(xs, mod_all, norm_w, w1, w2, l, half, split_out=False, side=()):
    j, pre, post = (0, 0, 1) if half == 0 else (6, 4, 5)
    tm = FFN_TM
    n_tiles = T_ALL // tm
    n_ctx = T_CTX // tm
    n_lat = T_LAT // tm
    split_in = len(xs) == 2
    tile = (tm, D_MODEL)

    def specs(shift):
        t = lambda i: jnp.clip(i + shift, 0, n_tiles - 1)
        if split_in:
            return [pl.BlockSpec(tile, lambda i: (jnp.minimum(t(i), n_ctx - 1), 0)),
                    pl.BlockSpec(tile, lambda i: (jnp.clip(t(i) - n_ctx, 0, n_lat - 1), 0))]
        return [pl.BlockSpec(tile, lambda i: (t(i), 0))]

    def mod_spec(shift):
        t = lambda i: jnp.clip(i + shift, 0, n_tiles - 1)
        return pl.BlockSpec((None, 1, N_MOD, D_MODEL), lambda i: (l, _cond_row(t(i), tm), 0, 0))

    prev = lambda i: jnp.clip(i - 1, 0, n_tiles - 1)
    if split_out:
        out_shape = [jax.ShapeDtypeStruct((T_CTX, D_MODEL), F32),
                     jax.ShapeDtypeStruct((T_LAT, D_MODEL), F32)]
        out_specs = [pl.BlockSpec(tile, lambda i: (jnp.minimum(prev(i), n_ctx - 1), 0)),
                     pl.BlockSpec(tile, lambda i: (jnp.clip(prev(i) - n_ctx, 0, n_lat - 1), 0))]
    else:
        out_shape = [jax.ShapeDtypeStruct((T_ALL, D_MODEL), F32)]
        out_specs = [pl.BlockSpec(tile, lambda i: (prev(i), 0))]
    scratch = [pltpu.VMEM(tile, BF16), pltpu.VMEM(tile, BF16), pltpu.VMEM(tile, F32)]
    if split_out:
        scratch.append(pltpu.VMEM(tile, F32))
    whole = lambda i: (0, 0)
    outs = pl.pallas_call(
        functools.partial(_ffn_kernel, j=j, pre=pre, post=post, split_in=split_in,
                          split_out=split_out, n_side=len(side)),
        out_shape=out_shape + [s[3] for s in side],
        grid=(n_tiles + 1,),
        in_specs=specs(-1) + specs(1) + [
            mod_spec(-1),
            mod_spec(1),
            pl.BlockSpec((None, 6, D_MODEL), lambda i: (l, 0, 0)),
            pl.BlockSpec((D_MODEL, 2 * D_FF), whole, pipeline_mode=pl.Buffered(1)),
            pl.BlockSpec((D_FF, D_MODEL), whole, pipeline_mode=pl.Buffered(1)),
        ] + [s[1] for s in side],
        out_specs=out_specs + [s[2] for s in side],
        scratch_shapes=scratch,
        compiler_params=pltpu.CompilerParams(
            dimension_semantics=("arbitrary",), vmem_limit_bytes=VMEM_LIMIT),
        name="ffn_half",
    )(*xs, *xs, mod_all, mod_all, norm_w, w1, w2, *[s[0] for s in side])
    n_o = len(out_shape)
    return outs[:n_o], outs[n_o:]


def _swap16(x):
    w = x.shape[-1]
    lane = lax.broadcasted_iota(jnp.int32, x.shape, x.ndim - 1)
    first = (lane & 31) < 16
    return jnp.where(first, pltpu.roll(x, w - 16, axis=x.ndim - 1), pltpu.roll(x, 16, axis=x.ndim - 1))


def _rope(x, cos_t, sin_t):
    reps = x.shape[-1] // LANES
    c = jnp.concatenate([cos_t] * reps, axis=-1) if reps > 1 else cos_t
    s = jnp.concatenate([sin_t] * reps, axis=-1) if reps > 1 else sin_t
    return x * c + _swap16(x) * s


def _proj_kernel(x_ref, mod_ref, nw_ref, w_ref, cos_ref, sin_ref, ws_ref, bz_ref, *rest):
    q_ref, kc_ref, vc_ref, kl_ref, vl_ref, a_ref, p_ref = rest[-7:]
    i = pl.program_id(0)
    tm = x_ref.shape[0]
    x = x_ref[...]
    shift = mod_ref[0, 3:4, :]
    scale = mod_ref[0, 4:5, :]
    h = (_rms(x, nw_ref[2:3, :]) * (1 + scale) + shift).astype(BF16)
    proj = _dot(h, w_ref[...])
    q = proj[:, 0:ATTN_WIDTH] * ATTN_SCALE
    k = proj[:, ATTN_WIDTH:ATTN_WIDTH + KV_WIDTH]
    v = proj[:, ATTN_WIDTH + KV_WIDTH:ATTN_WIDTH + 2 * KV_WIDTH]
    gu = proj[:, 768:1024]
    gv = proj[:, 1024:1280]
    p_ref[...] = proj[:, 1280:1536]

    is_lat = i >= T_CTX // tm

    @pl.when(is_lat)
    def _():
        cos_t = cos_ref[...]
        sin_t = sin_ref[...]
        q_ref[...] = _rope(q, cos_t, sin_t).astype(BF16)
        kl_ref[...] = _rope(k, cos_t, sin_t)
        vl_ref[...] = v

    @pl.when(jnp.logical_not(is_lat))
    def _():
        q_ref[...] = q.astype(BF16)
        for s in range(tm // CTX_SEQ):
            kc_ref[s] = k[s * CTX_SEQ:(s + 1) * CTX_SEQ, :]
            vc_ref[s] = v[s * CTX_SEQ:(s + 1) * CTX_SEQ, :]

    r = lax.broadcasted_iota(jnp.int32, (GM_WIDTH, GM_WIDTH), 0)
    c = lax.broadcasted_iota(jnp.int32, (GM_WIDTH, GM_WIDTH), 1)
    grp_mean = jnp.where((r >> 6) == (c >> 6), 1.0 / GM_DIM, 0.0).astype(BF16)
    sq = gv * gv
    sq_hi = sq.astype(BF16)
    sq_lo = (sq - sq_hi.astype(F32)).astype(BF16)
    ms = _dot(sq_hi, grp_mean) + _dot(sq_lo, grp_mean)
    vh = (gv * lax.rsqrt(ms + EPS)).astype(BF16)
    lane = lax.broadcasted_iota(jnp.int32, (CHUNK, GM_WIDTH), 1)
    ws = ws_ref[...]
    bz = bz_ref[...]
    for n in range(tm // CHUNK):
        vn = vh[n * CHUNK:(n + 1) * CHUNK, :]
        bd = jnp.concatenate(
            [jnp.where((lane >> 6) == hh, vn, jnp.zeros_like(vn)) for hh in range(GM_HEADS)], axis=0)
        z = _dot(ws, bd) + bz
        a_ref[n * CHUNK:(n + 1) * CHUNK, :] = (gu[n * CHUNK:(n + 1) * CHUNK, :] * z).astype(BF16)


def _mixer_proj(x, mod_all, norm_w, w_in, cos_t, sin_t, ws_cat, bz, l, caches=None):
    tm = PROJ_TM
    n_ctx = T_CTX // tm
    per_seq = LAT_SEQ // tm
    seq_per_tile = tm // CTX_SEQ

    def tbl_map(i):
        return (lax.rem(jnp.maximum(i - n_ctx, 0), per_seq), 0)

    row = lambda i: (i, 0)
    lat_row = lambda i: (jnp.maximum(i - n_ctx, 0), 0)
    cache_blk = pl.BlockSpec((seq_per_tile, None, CTX_SEQ, KV_WIDTH),
                             lambda i: (jnp.minimum(i, n_ctx - 1), l, 0, 0))
    cache_shape = jax.ShapeDtypeStruct((N_CTX_SEQ, DEPTH, CTX_SEQ, KV_WIDTH), F32)
    extra_specs, extra_args, aliases = [], [], {}
    if caches is not None:
        extra_specs = [pl.BlockSpec(memory_space=pl.ANY)] * 2
        extra_args = list(caches)
        aliases = {8: 1, 9: 2}
    return pl.pallas_call(
        _proj_kernel,
        out_shape=(
            jax.ShapeDtypeStruct((T_ALL, ATTN_WIDTH), BF16),
            cache_shape,
            cache_shape,
            jax.ShapeDtypeStruct((T_LAT, KV_WIDTH), F32),
            jax.ShapeDtypeStruct((T_LAT, KV_WIDTH), F32),
            jax.ShapeDtypeStruct((T_ALL, GM_WIDTH), BF16),
            jax.ShapeDtypeStruct((T_ALL, POOL_WIDTH), F32),
        ),
        grid=(T_ALL // tm,),
        input_output_aliases=aliases,
        in_specs=[
            pl.BlockSpec((tm, D_MODEL), row),
            pl.BlockSpec((None, 1, N_MOD, D_MODEL), lambda i: (l, _cond_row(i, tm), 0, 0)),
            pl.BlockSpec((None, 6, D_MODEL), lambda i: (l, 0, 0)),
            pl.BlockSpec((D_MODEL, IN_WIDTH), lambda i: (0, 0)),
            pl.BlockSpec((tm, LANES), tbl_map),
            pl.BlockSpec((tm, LANES), tbl_map),
            pl.BlockSpec((None, CHUNK, GM_HEADS * CHUNK), lambda i: (l, 0, 0)),
            pl.BlockSpec((None, CHUNK, GM_WIDTH), lambda i: (l, 0, 0)),
        ] + extra_specs,
        out_specs=(
            pl.BlockSpec((tm, ATTN_WIDTH), row),
            cache_blk,
            cache_blk,
            pl.BlockSpec((tm, KV_WIDTH), lat_row),
            pl.BlockSpec((tm, KV_WIDTH), lat_row),
            pl.BlockSpec((tm, GM_WIDTH), row),
            pl.BlockSpec((tm, POOL_WIDTH), row),
        ),
        compiler_params=pltpu.CompilerParams(
            dimension_semantics=("arbitrary",), vmem_limit_bytes=VMEM_LIMIT),
        name="mixer_proj",
    )(x, mod_all, norm_w, w_in, cos_t, sin_t, ws_cat, bz, *extra_args)


def _pool(x, wp, scale):
    s_len = x.shape[0]
    row = lax.broadcasted_iota(jnp.int32, x.shape, 0)
    lane = lax.broadcasted_iota(jnp.int32, x.shape, 1)

    def down(y, s):
        return jnp.where(row >= s, pltpu.roll(y, s, axis=0), 0.0)

    def up(y, s):
        return jnp.where(row < s_len - s, pltpu.roll(y, s_len - s, axis=0), 0.0)

    back = down(x, 1)
    fwd = x
    sums = [back + fwd]
    for a in POOL_HALF_WINDOWS[:-1]:
        back = back + down(back, a)
        fwd = fwd + up(fwd, a)
        sums.append(back + fwd)
    grp = lane >> 6
    tot = jnp.where(grp == 0, sums[0], jnp.where(grp == 1, sums[1], jnp.where(grp == 2, sums[2], sums[3])))
    half = jnp.where(grp == 0, POOL_HALF_WINDOWS[0],
                     jnp.where(grp == 1, POOL_HALF_WINDOWS[1],
                               jnp.where(grp == 2, POOL_HALF_WINDOWS[2], POOL_HALF_WINDOWS[3])))
    cnt = (jnp.minimum(row + half, s_len) - jnp.maximum(row - half, 0)).astype(F32)
    d = (tot / cnt - x).astype(BF16)
    return _dot(d, wp) * scale


def _dup_halves(t):
    lane = lax.broadcasted_iota(jnp.int32, t.shape, 1)
    lo = lane < HEAD_DIM
    sw = pltpu.roll(t, HEAD_DIM, axis=1)
    return (jnp.where(lo, t, sw).astype(BF16), jnp.where(lo, sw, t).astype(BF16))


def _core_kernel(*refs, latent, l):
    if latent:
        (x_ref, mod_ref, nw_ref, q_ref, k_ref, v_ref, a_ref, p_ref, ck_ref, cv_ref,
         wo_ref, wp_ref, ps_ref, sink_ref, o_ref, pool_ref, cat_ref) = refs
    else:
        (x_ref, mod_ref, nw_ref, q_ref, k_ref, v_ref, a_ref, p_ref,
         wo_ref, wp_ref, ps_ref, sink_ref, o_ref, pool_ref, cat_ref) = refs
    jq = pl.program_id(1)
    s_len = k_ref.shape[0]

    @pl.when(jq == 0)
    def _():
        pool_ref[...] = _pool(p_ref[...], wp_ref[...], ps_ref[...]).astype(BF16)

    if latent:
        start = jnp.clip(jq * Q_BLK - WINDOW, 0, s_len - BAND)
        start = pl.multiple_of(start, WINDOW)
        kk = _dup_halves(k_ref[pl.ds(start, BAND), :])
        vv = _dup_halves(v_ref[pl.ds(start, BAND), :])
        ckk = _dup_halves(ck_ref[...])
        cvv = _dup_halves(cv_ref[...])
        qpos = jq * Q_BLK + lax.broadcasted_iota(jnp.int32, (Q_BLK, BAND), 0)
        kpos = start + lax.broadcasted_iota(jnp.int32, (Q_BLK, BAND), 1)
        valid = jnp.abs(qpos - kpos) <= WINDOW
        segs = lambda g: [(kk[g], vv[g], valid), (ckk[g], cvv[g], None)]
    else:
        kk = _dup_halves(k_ref[...])
        vv = _dup_halves(v_ref[...])
        segs = lambda g: [(kk[g], vv[g], None)]

    lane = lax.broadcasted_iota(jnp.int32, (Q_BLK, LANES), 1)
    lo = lane < HEAD_DIM
    for g in range(N_KV_HEADS):
        for p in range(2):
            col = (2 * g + p) * LANES
            qpair = q_ref[:, col:col + LANES]
            out_pair = None
            for hf in range(2):
                sel = lo if hf == 0 else jnp.logical_not(lo)
                sink = sink_ref[l, g * 4 + p * 2 + hf]
                qm = jnp.where(sel, qpair, jnp.zeros_like(qpair))
                scores = []
                m = jnp.full((Q_BLK, 1), sink, F32)
                for keys, _, mask in segs(g):
                    s = _dot_t(qm, keys)
                    if mask is not None:
                        s = jnp.where(mask, s, NEG_INF)
                    scores.append(s)
                    m = jnp.maximum(m, jnp.max(s, axis=-1, keepdims=True))
                den = jnp.exp(sink - m)
                o = None
                for s, (_, vals, _) in zip(scores, segs(g)):
                    e = jnp.exp(s - m)
                    den = den + jnp.sum(e, axis=-1, keepdims=True)
                    part = _dot(e.astype(BF16), vals)
                    o = part if o is None else o + part
                o = o / den
                out_pair = o if out_pair is None else jnp.where(sel, o, out_pair)
            cat_ref[:, col:col + LANES] = out_pair.astype(BF16)

    cat_ref[:, ATTN_WIDTH:ATTN_WIDTH + GM_WIDTH] = a_ref[...]
    row0 = pl.multiple_of(jq * Q_BLK, Q_BLK)
    cat_ref[:, ATTN_WIDTH + GM_WIDTH:] = pool_ref[pl.ds(row0, Q_BLK), :]
    out = _dot(cat_ref[...], wo_ref[...])
    gate = mod_ref[0, 5:6, :]
    o_ref[...] = x_ref[...] + gate * _rms(out, nw_ref[3:4, :])


def _mixer_core(x, mod_all, norm_w, q, k, v, a, p, w_out, wp_bd, pool_scale, sink, l,
                cache=None, prev=None):
    latent = cache is not None
    if latent:
        n_seq, s_len, row_off, cond0 = N_LAT_SEQ, LAT_SEQ, T_CTX // Q_BLK, 1
    else:
        n_seq, s_len, row_off, cond0 = N_CTX_SEQ, CTX_SEQ, 0, 0
    nq = s_len // Q_BLK
    seq_off = row_off * Q_BLK // s_len

    blk = lambda b, j: (row_off + b * nq + j, 0)
    seq = lambda b, j: (seq_off + b, 0)
    cond = (lambda b, j: (l, cond0 + b, 0, 0)) if latent else (lambda b, j: (l, 0, 0, 0))
    if latent:
        kv_spec = pl.BlockSpec((s_len, KV_WIDTH), lambda b, j: (b, 0))
    else:
        kv_spec = pl.BlockSpec((None, None, s_len, KV_WIDTH), lambda b, j: (b, l, 0, 0))
    in_specs = [
        pl.BlockSpec((Q_BLK, D_MODEL), blk),
        pl.BlockSpec((None, 1, N_MOD, D_MODEL), cond),
        pl.BlockSpec((None, 6, D_MODEL), lambda b, j: (l, 0, 0)),
        pl.BlockSpec((Q_BLK, ATTN_WIDTH), blk),
        kv_spec,
        kv_spec,
        pl.BlockSpec((Q_BLK, GM_WIDTH), blk),
        pl.BlockSpec((s_len, POOL_WIDTH), seq),
    ]
    args = [x, mod_all, norm_w, q, k, v, a, p]
    if latent:
        ck, cv = cache
        in_specs += [
            pl.BlockSpec((None, None, PAST_LEN, KV_WIDTH), lambda b, j: (b, l, 0, 0)),
            pl.BlockSpec((None, None, PAST_LEN, KV_WIDTH), lambda b, j: (b, l, 0, 0)),
        ]
        args += [ck, cv]
    in_specs += [
        pl.BlockSpec((D_MODEL, D_MODEL), lambda b, j: (0, 0)),
        pl.BlockSpec((None, POOL_WIDTH, POOL_WIDTH), lambda b, j: (l, 0, 0)),
        pl.BlockSpec((None, 1, POOL_WIDTH), lambda b, j: (l, 0, 0)),
        pl.BlockSpec(memory_space=pltpu.SMEM),
    ]
    args += [w_out, wp_bd, pool_scale, sink]
    aliases = {}
    if prev is not None:
        in_specs.append(pl.BlockSpec(memory_space=pl.ANY))
        args.append(prev)
        aliases = {len(args) - 1: 0}

    kern = functools.partial(_core_kernel, latent=latent, l=l)
    if prev is not None:
        def kern(*refs, _k=kern):
            n_in = len(args)
            return _k(*refs[:n_in - 1], *refs[n_in:])

    return pl.pallas_call(
        kern,
        out_shape=jax.ShapeDtypeStruct((T_ALL, D_MODEL), F32),
        grid=(n_seq, nq),
        in_specs=in_specs,
        out_specs=pl.BlockSpec((Q_BLK, D_MODEL), blk),
        scratch_shapes=[
            pltpu.VMEM((s_len, POOL_WIDTH), BF16),
            pltpu.VMEM((Q_BLK, D_MODEL), BF16),
        ],
        input_output_aliases=aliases,
        compiler_params=pltpu.CompilerParams(
            dimension_semantics=("arbitrary", "arbitrary"), vmem_limit_bytes=VMEM_LIMIT),
        name="mixer_core_lat" if latent else "mixer_core_ctx",
    )(*args)


def _rope_tables():
    t = jnp.arange(LAT_SEQ, dtype=jnp.int32)
    row = (t // GRID_W).astype(F32)
    col = (t % GRID_W).astype(F32)
    nf = HEAD_DIM // 4
    inv = ROPE_BASE ** (-jnp.arange(nf, dtype=F32) / nf)
    ar = row[:, None] * inv
    ac = col[:, None] * inv
    cos64 = jnp.concatenate([jnp.cos(ar), jnp.cos(ar), jnp.cos(ac), jnp.cos(ac)], axis=-1)
    sin64 = jnp.concatenate([-jnp.sin(ar), jnp.sin(ar), -jnp.sin(ac), jnp.sin(ac)], axis=-1)
    return jnp.tile(cos64, (1, 2)), jnp.tile(sin64, (1, 2))


def kernel(x_prompt, x_sample, cache_k, cache_v, c, c_ctx, w_mod, b_mod, norm_w, w_in, w_out,
           attn_sink, w_spatial, b_spatial, w_pool, pool_scale, ffn_w1, ffn_w2):
    cond = jnp.concatenate(
        [c_ctx[None, :], c, jnp.zeros((N_COND - 1 - N_LAT_SEQ, D_MODEL), F32)], axis=0)
    mod_all = _adaln(cond, w_mod, b_mod).reshape(DEPTH, N_COND, N_MOD, D_MODEL)

    w1 = ffn_w1[0, 0].astype(BF16)
    w2 = ffn_w2[0, 0].astype(BF16)
    w_in_b = w_in[0].astype(BF16)
    ws_cat = w_spatial.transpose(0, 2, 1, 3).reshape(DEPTH, CHUNK, GM_HEADS * CHUNK).astype(BF16)
    bz = jnp.repeat(b_spatial.transpose(0, 2, 1), GM_DIM, axis=2)
    eye = jnp.eye(len(POOL_HALF_WINDOWS), dtype=F32)
    wp_bd = jnp.einsum('lgij,gh->lgihj', w_pool, eye).reshape(DEPTH, POOL_WIDTH, POOL_WIDTH).astype(BF16)
    ps = pool_scale.reshape(DEPTH, 1, POOL_WIDTH)
    cos_t, sin_t = _rope_tables()
    ck = cache_k.reshape(N_LAT_SEQ, DEPTH, PAST_LEN, KV_WIDTH)
    cv = cache_v.reshape(N_LAT_SEQ, DEPTH, PAST_LEN, KV_WIDTH)

    xs = (x_prompt.reshape(T_CTX, D_MODEL), x_sample.reshape(T_LAT, D_MODEL))
    caches = None
    for l in range(DEPTH):
        side = (_side_job(ffn_w1, (l, 1), W1_SIDE_BLOCKS), _side_job(ffn_w2, (l, 1), W2_SIDE_BLOCKS),
                _side_job(w_out, (l,), WIO_SIDE_BLOCKS))
        (x,), (w1, w2, w_out_b) = _ffn_half(xs, mod_all, norm_w, w1, w2, l, 0, side=side)
        q, kc, vc, kl, vl, a, p = _mixer_proj(
            x, mod_all, norm_w, w_in_b, cos_t, sin_t, ws_cat, bz, l, caches=caches)
        caches = (kc, vc)
        y = _mixer_core(x, mod_all, norm_w, q, kc, vc, a, p, w_out_b, wp_bd, ps, attn_sink, l)
        x = _mixer_core(x, mod_all, norm_w, q, kl, vl, a, p, w_out_b, wp_bd, ps, attn_sink, l,
                        cache=(ck, cv), prev=y)
        if l < DEPTH - 1:
            side = (_side_job(ffn_w1, (l + 1, 0), W1_SIDE_BLOCKS),
                    _side_job(ffn_w2, (l + 1, 0), W2_SIDE_BLOCKS),
                    _side_job(w_in, (l + 1,), WIO_SIDE_BLOCKS))
            xs, (w1, w2, w_in_b) = _ffn_half((x,), mod_all, norm_w, w1, w2, l, 1, side=side)
        else:
            xs, _ = _ffn_half((x,), mod_all, norm_w, w1, w2, l, 1, split_out=True)
    y_prompt = xs[0].reshape(N_CTX_SEQ, CTX_SEQ, D_MODEL)
    y_sample = xs[1].reshape(N_LAT_SEQ, LAT_SEQ, D_MODEL)
    cache_shape = (N_CTX_SEQ, DEPTH, CTX_SEQ, N_KV_HEADS, HEAD_DIM)
    return y_prompt, y_sample, caches[0].reshape(cache_shape), caches[1].reshape(cache_shape)
```

```python
import functools

import jax
import jax.numpy as jnp
from jax import lax
from jax.experimental import pallas as pl
from jax.experimental.pallas import tpu as pltpu

D_MODEL = 1024
N_CTX_SEQ = 32
CTX_SEQ = 256
DEPTH = 4
N_LAT_SEQ = 2
LAT_SEQ = 2048
PAST_LEN = 512
GRID_W = 64
HEAD_DIM = 64
ATTN_WIDTH = 512
N_HEADS = 8
N_KV_HEADS = 2
KV_WIDTH = 128
WINDOW = 128
GM_WIDTH = 256
GM_HEADS = 4
GM_DIM = 64
CHUNK = 128
POOL_WIDTH = 256
POOL_HALF_WINDOWS = (1, 2, 4, 8)
POOL_DIM = 64
IN_WIDTH = 1536
D_FF = 2816
N_MOD = 9
EPS = 1e-6
ROPE_BASE = 10000.0
NEG_INF = -1e30
ATTN_SCALE = HEAD_DIM ** -0.5

T_CTX = N_CTX_SEQ * CTX_SEQ
T_LAT = N_LAT_SEQ * LAT_SEQ
T_ALL = T_CTX + T_LAT
N_COND = 8

LANES = 128
VMEM_LIMIT = 56 * 1024 * 1024

FFN_TM = 512
FFN_CHUNKS = (512, 512, 512, 512, 512, 256)
W1_SIDE_BLOCKS = 16
W2_SIDE_BLOCKS = 22
WIO_SIDE_BLOCKS = 16
PROJ_TM = 1024
Q_BLK = 256
BAND = Q_BLK + 2 * WINDOW
CORE_CTX_ROWS = 1024
CORE_LAT_ROWS = 512
LOG2E = 1.4426950408889634
MOD_TN = 2304

BF16 = jnp.bfloat16
F32 = jnp.float32


def _dot(a, b):
    return jnp.dot(a, b, preferred_element_type=F32)


def _dot_t(a, b):
    return lax.dot_general(a, b, (((1,), (1,)), ((), ())), preferred_element_type=F32)


def _rms(x, g):
    return x * lax.rsqrt(jnp.mean(x * x, axis=-1, keepdims=True) + EPS) * g


def _zero_from(v):
    bits = lax.bitcast_convert_type(v, jnp.uint32)
    return ((bits >> 16) >> 16).astype(F32)


def _cond_row(tile, tm):
    n_ctx = T_CTX // tm
    per_seq = LAT_SEQ // tm
    return jnp.where(tile < n_ctx, 0, 1 + (tile - n_ctx) // per_seq)


def _mod_kernel(c_ref, w_ref, b_ref, o_ref):
    c = c_ref[...]
    s = jax.nn.silu(c).astype(BF16)
    o_ref[...] = _dot(s, w_ref[...].astype(BF16)) + b_ref[...]


def _adaln(cond, w_mod, b_mod):
    n_out = N_MOD * D_MODEL
    return pl.pallas_call(
        _mod_kernel,
        out_shape=jax.ShapeDtypeStruct((DEPTH, N_COND, n_out), F32),
        grid=(DEPTH, n_out // MOD_TN),
        in_specs=[
            pl.BlockSpec((N_COND, D_MODEL), lambda l, n: (0, 0)),
            pl.BlockSpec((None, D_MODEL, MOD_TN), lambda l, n: (l, 0, n)),
            pl.BlockSpec((None, 1, MOD_TN), lambda l, n: (l, 0, n)),
        ],
        out_specs=pl.BlockSpec((None, N_COND, MOD_TN), lambda l, n: (l, 0, n)),
        compiler_params=pltpu.CompilerParams(
            dimension_semantics=("arbitrary", "arbitrary"), vmem_limit_bytes=VMEM_LIMIT),
        name="adaln",
    )(cond, w_mod, b_mod.reshape(DEPTH, 1, n_out))


def _ffn_kernel(*refs, j, pre, post, split_in, split_out, n_side):
    n_x = 2 if split_in else 1
    n_o = 2 if split_out else 1
    xp_refs = refs[:n_x]
    xn_refs = refs[n_x:2 * n_x]
    modp_ref, modn_ref, nw_ref, w1_ref, w2_ref = refs[2 * n_x:2 * n_x + 5]
    pos = 2 * n_x + 5
    side_in = refs[pos:pos + n_side]
    o_refs = refs[pos + n_side:pos + n_side + n_o]
    side_out = refs[pos + n_side + n_o:pos + 2 * n_side + n_o]
    scratch = refs[pos + 2 * n_side + n_o:]
    h_refs, acc_ref = scratch[:2], scratch[2]
    i = pl.program_id(0)
    n_tiles = T_ALL // FFN_TM
    n_ctx = T_CTX // FFN_TM

    def load_x(x_refs, tile, rows):
        if split_in:
            return jnp.where(tile < n_ctx, x_refs[0][rows, :], x_refs[1][rows, :])
        return x_refs[0][rows, :]

    def prologue(x_refs, mod_ref, tile, rows):
        shift = mod_ref[0, j:j + 1, :]
        scale = mod_ref[0, j + 1:j + 2, :]
        x = load_x(x_refs, tile, rows)
        return _rms(x, nw_ref[pre:pre + 1, :]) * (1 + scale) + shift

    def epilogue(rows):
        gate = modp_ref[0, j + 2:j + 3, :]
        y = (load_x(xp_refs, i - 1, rows)
             + 0.5 * gate * _rms(acc_ref[rows, :], nw_ref[post:post + 1, :]))
        if split_out:
            scratch[3][rows, :] = y
        else:
            o_refs[0][rows, :] = y
        return y

    def store_split():
        prev_ctx = i - 1 < n_ctx

        @pl.when(prev_ctx)
        def _():
            o_refs[0][...] = scratch[3][...]

        @pl.when(jnp.logical_not(prev_ctx))
        def _():
            o_refs[1][...] = scratch[3][...]

    all_rows = slice(0, FFN_TM)
    halves = (slice(0, FFN_TM // 2), slice(FFN_TM // 2, FFN_TM))

    @pl.when(i == 0)
    def _():
        h_refs[0][...] = prologue(xp_refs, modp_ref, 0, all_rows).astype(BF16)
        acc_ref[...] = jnp.zeros_like(acc_ref)

    def main(h_cur, h_nxt):
        def vpu_piece(c):
            if c in (0, 1):
                return _zero_from(epilogue(halves[c])[:, 0:1])
            if c in (2, 3):
                rows = halves[c - 2]
                hf = prologue(xn_refs, modn_ref, i + 1, rows)
                h_nxt[rows, :] = hf.astype(BF16)
                return _zero_from(hf[:, 0:1])
            if c == 4:
                for s_in, s_out in zip(side_in, side_out):
                    s_out[...] = s_in[...].astype(BF16)
            return None

        acc = None
        off = 0
        tie = None
        for c, tk in enumerate(FFN_CHUNKS):
            g = _dot(h_cur[...], w1_ref[:, off:off + tk])
            u = _dot(h_cur[...], w1_ref[:, D_FF + off:D_FF + off + tk])
            if tie is not None:
                tied = halves[(c - 1) % 2]
                u = jnp.concatenate(
                    [u[r, :] + tie if r is tied else u[r, :] for r in halves], axis=0)
            a = (jax.nn.silu(g) * u).astype(BF16)
            part = _dot(a, w2_ref[off:off + tk, :])
            acc = part if acc is None else acc + part
            off += tk
            tie = vpu_piece(c)
        acc_ref[...] = acc

    for par in range(2):
        @pl.when(jnp.logical_and(i < n_tiles, lax.rem(i, 2) == par))
        def _(par=par):
            main(h_refs[par], h_refs[1 - par])

    @pl.when(i == n_tiles)
    def _():
        epilogue(all_rows)

    if split_out:
        store_split()


def _side_job(arr, lead, n_blk):
    rows, cols = arr.shape[-2:]
    blk = rows // n_blk
    nones = (None,) * len(lead)
    in_spec = pl.BlockSpec(nones + (blk, cols), lambda i: lead + (jnp.minimum(i, n_blk - 1), 0))
    out_spec = pl.BlockSpec((blk, cols), lambda i: (jnp.minimum(i, n_blk - 1), 0))
    return arr, in_spec, out_spec, jax.ShapeDtypeStruct((rows, cols), BF16)


def _ffn_half(xs, mod_all, norm_w, w1, w2, l, half, split_out=False, side=()):
    j, pre, post = (0, 0, 1) if half == 0 else (6, 4, 5)
    tm = FFN_TM
    n_tiles = T_ALL // tm
    n_ctx = T_CTX // tm
    n_lat = T_LAT // tm
    split_in = len(xs) == 2
    tile = (tm, D_MODEL)

    def specs(shift):
        t = lambda i: jnp.clip(i + shift, 0, n_tiles - 1)
        if split_in:
            return [pl.BlockSpec(tile, lambda i: (jnp.minimum(t(i), n_ctx - 1), 0)),
                    pl.BlockSpec(tile, lambda i: (jnp.clip(t(i) - n_ctx, 0, n_lat - 1), 0))]
        return [pl.BlockSpec(tile, lambda i: (t(i), 0))]

    def mod_spec(shift):
        t = lambda i: jnp.clip(i + shift, 0, n_tiles - 1)
        return pl.BlockSpec((None, 1, N_MOD, D_MODEL), lambda i: (l, _cond_row(t(i), tm), 0, 0))

    prev = lambda i: jnp.clip(i - 1, 0, n_tiles - 1)
    if split_out:
        out_shape = [jax.ShapeDtypeStruct((T_CTX, D_MODEL), F32),
                     jax.ShapeDtypeStruct((T_LAT, D_MODEL), F32)]
        out_specs = [pl.BlockSpec(tile, lambda i: (jnp.minimum(prev(i), n_ctx - 1), 0)),
                     pl.BlockSpec(tile, lambda i: (jnp.clip(prev(i) - n_ctx, 0, n_lat - 1), 0))]
    else:
        out_shape = [jax.ShapeDtypeStruct((T_ALL, D_MODEL), F32)]
        out_specs = [pl.BlockSpec(tile, lambda i: (prev(i), 0))]
    scratch = [pltpu.VMEM(tile, BF16), pltpu.VMEM(tile, BF16), pltpu.VMEM(tile, F32)]
    if split_out:
        scratch.append(pltpu.VMEM(tile, F32))
    whole = lambda i: (0, 0)
    outs = pl.pallas_call(
        functools.partial(_ffn_kernel, j=j, pre=pre, post=post, split_in=split_in,
                          split_out=split_out, n_side=len(side)),
        out_shape=out_shape + [s[3] for s in side],
        grid=(n_tiles + 1,),
        in_specs=specs(-1) + specs(1) + [
            mod_spec(-1),
            mod_spec(1),
            pl.BlockSpec((None, 6, D_MODEL), lambda i: (l, 0, 0)),
            pl.BlockSpec((D_MODEL, 2 * D_FF), whole, pipeline_mode=pl.Buffered(1)),
            pl.BlockSpec((D_FF, D_MODEL), whole, pipeline_mode=pl.Buffered(1)),
        ] + [s[1] for s in side],
        out_specs=out_specs + [s[2] for s in side],
        scratch_shapes=scratch,
        compiler_params=pltpu.CompilerParams(
            dimension_semantics=("arbitrary",), vmem_limit_bytes=VMEM_LIMIT),
        name="ffn_half",
    )(*xs, *xs, mod_all, mod_all, norm_w, w1, w2, *[s[0] for s in side])
    n_o = len(out_shape)
    return outs[:n_o], outs[n_o:]


def _swap16(x):
    w = x.shape[-1]
    lane = lax.broadcasted_iota(jnp.int32, x.shape, x.ndim - 1)
    first = (lane & 31) < 16
    return jnp.where(first, pltpu.roll(x, w - 16, axis=x.ndim - 1), pltpu.roll(x, 16, axis=x.ndim - 1))


def _rope(x, cos_t, sin_t):
    reps = x.shape[-1] // LANES
    c = jnp.concatenate([cos_t] * reps, axis=-1) if reps > 1 else cos_t
    s = jnp.concatenate([sin_t] * reps, axis=-1) if reps > 1 else sin_t
    return x * c + _swap16(x) * s


def _proj_kernel(x_ref, mod_ref, nw_ref, w_ref, cos_ref, sin_ref, ws_ref, bz_ref, *rest):
    q_ref, kc_ref, vc_ref, kl_ref, vl_ref, a_ref, p_ref = rest[-7:]
    i = pl.program_id(0)
    tm = x_ref.shape[0]
    x = x_ref[...]
    shift = mod_ref[0, 3:4, :]
    scale = mod_ref[0, 4:5, :]
    h = (_rms(x, nw_ref[2:3, :]) * (1 + scale) + shift).astype(BF16)
    proj = _dot(h, w_ref[...])
    q = proj[:, 0:ATTN_WIDTH] * (ATTN_SCALE * LOG2E)
    k = proj[:, ATTN_WIDTH:ATTN_WIDTH + KV_WIDTH]
    v = proj[:, ATTN_WIDTH + KV_WIDTH:ATTN_WIDTH + 2 * KV_WIDTH]
    gu = proj[:, 768:1024]
    gv = proj[:, 1024:1280]
    p_ref[...] = proj[:, 1280:1536]

    is_lat = i >= T_CTX // tm

    @pl.when(is_lat)
    def _():
        cos_t = cos_ref[...]
        sin_t = sin_ref[...]
        q_ref[...] = _rope(q, cos_t, sin_t).astype(BF16)
        kl_ref[...] = _rope(k, cos_t, sin_t)
        vl_ref[...] = v

    @pl.when(jnp.logical_not(is_lat))
    def _():
        q_ref[...] = q.astype(BF16)
        for s in range(tm // CTX_SEQ):
            kc_ref[s] = k[s * CTX_SEQ:(s + 1) * CTX_SEQ, :]
            vc_ref[s] = v[s * CTX_SEQ:(s + 1) * CTX_SEQ, :]

    r = lax.broadcasted_iota(jnp.int32, (GM_WIDTH, GM_WIDTH), 0)
    c = lax.broadcasted_iota(jnp.int32, (GM_WIDTH, GM_WIDTH), 1)
    grp_mean = jnp.where((r >> 6) == (c >> 6), 1.0 / GM_DIM, 0.0).astype(BF16)
    sq = gv * gv
    sq_hi = sq.astype(BF16)
    sq_lo = (sq - sq_hi.astype(F32)).astype(BF16)
    ms = _dot(sq_hi, grp_mean) + _dot(sq_lo, grp_mean)
    vh = (gv * lax.rsqrt(ms + EPS)).astype(BF16)
    lane = lax.broadcasted_iota(jnp.int32, (CHUNK, GM_WIDTH), 1)
    ws = ws_ref[...]
    bz = bz_ref[...]
    for n in range(tm // CHUNK):
        vn = vh[n * CHUNK:(n + 1) * CHUNK, :]
        bd = jnp.concatenate(
            [jnp.where((lane >> 6) == hh, vn, jnp.zeros_like(vn)) for hh in range(GM_HEADS)], axis=0)
        z = _dot(ws, bd) + bz
        a_ref[n * CHUNK:(n + 1) * CHUNK, :] = (gu[n * CHUNK:(n + 1) * CHUNK, :] * z).astype(BF16)


def _mixer_proj(x, mod_all, norm_w, w_in, cos_t, sin_t, ws_cat, bz, l, caches=None):
    tm = PROJ_TM
    n_ctx = T_CTX // tm
    per_seq = LAT_SEQ // tm
    seq_per_tile = tm // CTX_SEQ

    def tbl_map(i):
        return (lax.rem(jnp.maximum(i - n_ctx, 0), per_seq), 0)

    row = lambda i: (i, 0)
    lat_row = lambda i: (jnp.maximum(i - n_ctx, 0), 0)
    cache_blk = pl.BlockSpec((seq_per_tile, None, CTX_SEQ, KV_WIDTH),
                             lambda i: (jnp.minimum(i, n_ctx - 1), l, 0, 0))
    cache_shape = jax.ShapeDtypeStruct((N_CTX_SEQ, DEPTH, CTX_SEQ, KV_WIDTH), F32)
    extra_specs, extra_args, aliases = [], [], {}
    if caches is not None:
        extra_specs = [pl.BlockSpec(memory_space=pl.ANY)] * 2
        extra_args = list(caches)
        aliases = {8: 1, 9: 2}
    return pl.pallas_call(
        _proj_kernel,
        out_shape=(
            jax.ShapeDtypeStruct((T_ALL, ATTN_WIDTH), BF16),
            cache_shape,
            cache_shape,
            jax.ShapeDtypeStruct((T_LAT, KV_WIDTH), F32),
            jax.ShapeDtypeStruct((T_LAT, KV_WIDTH), F32),
            jax.ShapeDtypeStruct((T_ALL, GM_WIDTH), BF16),
            jax.ShapeDtypeStruct((T_ALL, POOL_WIDTH), F32),
        ),
        grid=(T_ALL // tm,),
        input_output_aliases=aliases,
        in_specs=[
            pl.BlockSpec((tm, D_MODEL), row),
            pl.BlockSpec((None, 1, N_MOD, D_MODEL), lambda i: (l, _cond_row(i, tm), 0, 0)),
            pl.BlockSpec((None, 6, D_MODEL), lambda i: (l, 0, 0)),
            pl.BlockSpec((D_MODEL, IN_WIDTH), lambda i: (0, 0)),
            pl.BlockSpec((tm, LANES), tbl_map),
            pl.BlockSpec((tm, LANES), tbl_map),
            pl.BlockSpec((None, CHUNK, GM_HEADS * CHUNK), lambda i: (l, 0, 0)),
            pl.BlockSpec((None, CHUNK, GM_WIDTH), lambda i: (l, 0, 0)),
        ] + extra_specs,
        out_specs=(
            pl.BlockSpec((tm, ATTN_WIDTH), row),
            cache_blk,
            cache_blk,
            pl.BlockSpec((tm, KV_WIDTH), lat_row),
            pl.BlockSpec((tm, KV_WIDTH), lat_row),
            pl.BlockSpec((tm, GM_WIDTH), row),
            pl.BlockSpec((tm, POOL_WIDTH), row),
        ),
        compiler_params=pltpu.CompilerParams(
            dimension_semantics=("arbitrary",), vmem_limit_bytes=VMEM_LIMIT),
        name="mixer_proj",
    )(x, mod_all, norm_w, w_in, cos_t, sin_t, ws_cat, bz, *extra_args)


def _pool(x, wp, scale):
    s_len = x.shape[0]
    row = lax.broadcasted_iota(jnp.int32, x.shape, 0)
    lane = lax.broadcasted_iota(jnp.int32, x.shape, 1)

    def down(y, s):
        return jnp.where(row >= s, pltpu.roll(y, s, axis=0), 0.0)

    def up(y, s):
        return jnp.where(row < s_len - s, pltpu.roll(y, s_len - s, axis=0), 0.0)

    back = down(x, 1)
    fwd = x
    sums = [back + fwd]
    for a in POOL_HALF_WINDOWS[:-1]:
        back = back + down(back, a)
        fwd = fwd + up(fwd, a)
        sums.append(back + fwd)
    grp = lane >> 6
    tot = jnp.where(grp == 0, sums[0], jnp.where(grp == 1, sums[1], jnp.where(grp == 2, sums[2], sums[3])))
    half = jnp.where(grp == 0, POOL_HALF_WINDOWS[0],
                     jnp.where(grp == 1, POOL_HALF_WINDOWS[1],
                               jnp.where(grp == 2, POOL_HALF_WINDOWS[2], POOL_HALF_WINDOWS[3])))
    cnt = (jnp.minimum(row + half, s_len) - jnp.maximum(row - half, 0)).astype(F32)
    d = (tot / cnt - x).astype(BF16)
    return _dot(d, wp) * scale


def _dup_halves(t):
    lane = lax.broadcasted_iota(jnp.int32, t.shape, 1)
    lo = lane < HEAD_DIM
    sw = pltpu.roll(t, HEAD_DIM, axis=1)
    return (jnp.where(lo, t, sw).astype(BF16), jnp.where(lo, sw, t).astype(BF16))


def _head_values(v):
    lane = lax.broadcasted_iota(jnp.int32, v.shape, 1)
    lo = lane < HEAD_DIM
    sw = pltpu.roll(v, HEAD_DIM, axis=1)
    one = jnp.ones_like(v)
    pick = lambda a, b: jnp.where(lo, a, b).astype(BF16)
    return ((pick(v, one), pick(one, sw)), (pick(sw, one), pick(one, v)))


def _attend(q_ref, r0, segs, sink_ref, l, cat_ref):
    lane = lax.broadcasted_iota(jnp.int32, (Q_BLK, LANES), 1)
    lo = lane < HEAD_DIM
    rows = slice(r0, r0 + Q_BLK)
    for g in range(N_KV_HEADS):
        for p in range(2):
            col = (2 * g + p) * LANES
            qpair = q_ref[rows, col:col + LANES]
            outs, sink_terms = [], []
            for hf in range(2):
                sel = lo if hf == 0 else jnp.logical_not(lo)
                sink = sink_ref[l, g * 4 + p * 2 + hf] * LOG2E
                qm = jnp.where(sel, qpair, jnp.zeros_like(qpair))
                scores = []
                m = jnp.full((Q_BLK, 1), sink, F32)
                for keys, _, bias in segs(g):
                    s = _dot_t(qm, keys)
                    if bias is not None:
                        s = s + bias
                    scores.append(s)
                    m = jnp.maximum(m, jnp.max(s, axis=-1, keepdims=True))
                o = None
                for s, (_, vals, _) in zip(scores, segs(g)):
                    part = _dot(jnp.exp2(s - m).astype(BF16), vals[hf])
                    o = part if o is None else o + part
                outs.append(o)
                sink_terms.append(jnp.exp2(sink - m))
            num = jnp.where(lo, outs[0], outs[1])
            den = jnp.where(lo, outs[1] + sink_terms[1], outs[0] + sink_terms[0])
            cat_ref[rows, col:col + LANES] = (num / pltpu.roll(den, HEAD_DIM, axis=1)).astype(BF16)


def _core_kernel(*refs, latent, l):
    if latent:
        (x_ref, mod_ref, nw_ref, q_ref, k_ref, v_ref, a_ref, p_ref, ck_ref, cv_ref,
         wo_ref, wp_ref, ps_ref, sink_ref, o_ref, cat_ref, pool_ref) = refs
    else:
        (x_ref, mod_ref, nw_ref, q_ref, k_ref, v_ref, a_ref, p_ref,
         wo_ref, wp_ref, ps_ref, sink_ref, o_ref, cat_ref) = refs
    n_rows = x_ref.shape[0]
    pool_cols = slice(ATTN_WIDTH + GM_WIDTH, D_MODEL)

    if latent:
        jq = pl.program_id(1)
        s_len = k_ref.shape[0]

        @pl.when(jq == 0)
        def _():
            pool_ref[...] = _pool(p_ref[...], wp_ref[...], ps_ref[...]).astype(BF16)

        ckk = _dup_halves(ck_ref[...])
        cvv = _head_values(cv_ref[...])
        for s in range(n_rows // Q_BLK):
            q0 = jq * n_rows + s * Q_BLK
            start = pl.multiple_of(jnp.clip(q0 - WINDOW, 0, s_len - BAND), WINDOW)
            kk = _dup_halves(k_ref[pl.ds(start, BAND), :])
            vv = _head_values(v_ref[pl.ds(start, BAND), :])
            qpos = q0 + lax.broadcasted_iota(jnp.int32, (Q_BLK, BAND), 0)
            kpos = start + lax.broadcasted_iota(jnp.int32, (Q_BLK, BAND), 1)
            bias = jnp.where(jnp.abs(qpos - kpos) <= WINDOW, 0.0, NEG_INF)
            _attend(q_ref, s * Q_BLK, lambda g: [(kk[g], vv[g], bias), (ckk[g], cvv[g], None)],
                    sink_ref, l, cat_ref)
        cat_ref[:, pool_cols] = pool_ref[pl.ds(pl.multiple_of(jq * n_rows, n_rows), n_rows), :]
    else:
        for s in range(n_rows // Q_BLK):
            rows = slice(s * Q_BLK, (s + 1) * Q_BLK)
            kk = _dup_halves(k_ref[s])
            vv = _head_values(v_ref[s])
            cat_ref[rows, pool_cols] = _pool(p_ref[rows, :], wp_ref[...], ps_ref[...]).astype(BF16)
            _attend(q_ref, s * Q_BLK, lambda g: [(kk[g], vv[g], None)], sink_ref, l, cat_ref)

    cat_ref[:, ATTN_WIDTH:ATTN_WIDTH + GM_WIDTH] = a_ref[...]
    out = _dot(cat_ref[...], wo_ref[...])
    gate = mod_ref[0, 5:6, :]
    o_ref[...] = x_ref[...] + gate * _rms(out, nw_ref[3:4, :])


def _mixer_core(x, mod_all, norm_w, q, k, v, a, p, w_out, wp_bd, pool_scale, sink, l,
                cache=None, prev=None):
    latent = cache is not None
    if latent:
        n_rows = CORE_LAT_ROWS
        grid = (N_LAT_SEQ, LAT_SEQ // n_rows)
        blk = lambda b, j: (T_CTX // n_rows + b * grid[1] + j, 0)
        cond = lambda b, j: (l, 1 + b, 0, 0)
        kv_spec = pl.BlockSpec((LAT_SEQ, KV_WIDTH), lambda b, j: (b, 0))
        p_spec = pl.BlockSpec((LAT_SEQ, POOL_WIDTH), lambda b, j: (T_CTX // LAT_SEQ + b, 0))
    else:
        n_rows = CORE_CTX_ROWS
        grid = (T_CTX // n_rows, 1)
        blk = lambda b, j: (b, 0)
        cond = lambda b, j: (l, 0, 0, 0)
        kv_spec = pl.BlockSpec((n_rows // CTX_SEQ, None, CTX_SEQ, KV_WIDTH),
                               lambda b, j: (b, l, 0, 0))
        p_spec = pl.BlockSpec((n_rows, POOL_WIDTH), blk)
    in_specs = [
        pl.BlockSpec((n_rows, D_MODEL), blk),
        pl.BlockSpec((None, 1, N_MOD, D_MODEL), cond),
        pl.BlockSpec((None, 6, D_MODEL), lambda b, j: (l, 0, 0)),
        pl.BlockSpec((n_rows, ATTN_WIDTH), blk),
        kv_spec,
        kv_spec,
        pl.BlockSpec((n_rows, GM_WIDTH), blk),
        p_spec,
    ]
    args = [x, mod_all, norm_w, q, k, v, a, p]
    if latent:
        ck, cv = cache
        in_specs += [
            pl.BlockSpec((None, None, PAST_LEN, KV_WIDTH), lambda b, j: (b, l, 0, 0)),
            pl.BlockSpec((None, None, PAST_LEN, KV_WIDTH), lambda b, j: (b, l, 0, 0)),
        ]
        args += [ck, cv]
    in_specs += [
        pl.BlockSpec((D_MODEL, D_MODEL), lambda b, j: (0, 0)),
        pl.BlockSpec((None, POOL_WIDTH, POOL_WIDTH), lambda b, j: (l, 0, 0)),
        pl.BlockSpec((None, 1, POOL_WIDTH), lambda b, j: (l, 0, 0)),
        pl.BlockSpec(memory_space=pltpu.SMEM),
    ]
    args += [w_out, wp_bd, pool_scale, sink]
    aliases = {}
    if prev is not None:
        in_specs.append(pl.BlockSpec(memory_space=pl.ANY))
        args.append(prev)
        aliases = {len(args) - 1: 0}

    kern = functools.partial(_core_kernel, latent=latent, l=l)
    if prev is not None:
        def kern(*refs, _k=kern):
            n_in = len(args)
            return _k(*refs[:n_in - 1], *refs[n_in:])

    scratch = [pltpu.VMEM((n_rows, D_MODEL), BF16)]
    if latent:
        scratch.append(pltpu.VMEM((LAT_SEQ, POOL_WIDTH), BF16))
    return pl.pallas_call(
        kern,
        out_shape=jax.ShapeDtypeStruct((T_ALL, D_MODEL), F32),
        grid=grid,
        in_specs=in_specs,
        out_specs=pl.BlockSpec((n_rows, D_MODEL), blk),
        scratch_shapes=scratch,
        input_output_aliases=aliases,
        compiler_params=pltpu.CompilerParams(
            dimension_semantics=("arbitrary", "arbitrary"), vmem_limit_bytes=VMEM_LIMIT),
        name="mixer_core_lat" if latent else "mixer_core_ctx",
    )(*args)


def _rope_tables():
    t = jnp.arange(LAT_SEQ, dtype=jnp.int32)
    row = (t // GRID_W).astype(F32)
    col = (t % GRID_W).astype(F32)
    nf = HEAD_DIM // 4
    inv = ROPE_BASE ** (-jnp.arange(nf, dtype=F32) / nf)
    ar = row[:, None] * inv
    ac = col[:, None] * inv
    cos64 = jnp.concatenate([jnp.cos(ar), jnp.cos(ar), jnp.cos(ac), jnp.cos(ac)], axis=-1)
    sin64 = jnp.concatenate([-jnp.sin(ar), jnp.sin(ar), -jnp.sin(ac), jnp.sin(ac)], axis=-1)
    return jnp.tile(cos64, (1, 2)), jnp.tile(sin64, (1, 2))


def kernel(x_prompt, x_sample, cache_k, cache_v, c, c_ctx, w_mod, b_mod, norm_w, w_in, w_out,
           attn_sink, w_spatial, b_spatial, w_pool, pool_scale, ffn_w1, ffn_w2):
    cond = jnp.concatenate(
        [c_ctx[None, :], c, jnp.zeros((N_COND - 1 - N_LAT_SEQ, D_MODEL), F32)], axis=0)
    mod_all = _adaln(cond, w_mod, b_mod).reshape(DEPTH, N_COND, N_MOD, D_MODEL)

    w1 = ffn_w1[0, 0].astype(BF16)
    w2 = ffn_w2[0, 0].astype(BF16)
    w_in_b = w_in[0].astype(BF16)
    ws_cat = w_spatial.transpose(0, 2, 1, 3).reshape(DEPTH, CHUNK, GM_HEADS * CHUNK).astype(BF16)
    bz = jnp.repeat(b_spatial.transpose(0, 2, 1), GM_DIM, axis=2)
    eye = jnp.eye(len(POOL_HALF_WINDOWS), dtype=F32)
    wp_bd = jnp.einsum('lgij,gh->lgihj', w_pool, eye).reshape(DEPTH, POOL_WIDTH, POOL_WIDTH).astype(BF16)
    ps = pool_scale.reshape(DEPTH, 1, POOL_WIDTH)
    cos_t, sin_t = _rope_tables()
    ck = cache_k.reshape(N_LAT_SEQ, DEPTH, PAST_LEN, KV_WIDTH)
    cv = cache_v.reshape(N_LAT_SEQ, DEPTH, PAST_LEN, KV_WIDTH)

    xs = (x_prompt.reshape(T_CTX, D_MODEL), x_sample.reshape(T_LAT, D_MODEL))
    caches = None
    for l in range(DEPTH):
        side = (_side_job(ffn_w1, (l, 1), W1_SIDE_BLOCKS), _side_job(ffn_w2, (l, 1), W2_SIDE_BLOCKS),
                _side_job(w_out, (l,), WIO_SIDE_BLOCKS))
        (x,), (w1, w2, w_out_b) = _ffn_half(xs, mod_all, norm_w, w1, w2, l, 0, side=side)
        q, kc, vc, kl, vl, a, p = _mixer_proj(
            x, mod_all, norm_w, w_in_b, cos_t, sin_t, ws_cat, bz, l, caches=caches)
        caches = (kc, vc)
        y = _mixer_core(x, mod_all, norm_w, q, kc, vc, a, p, w_out_b, wp_bd, ps, attn_sink, l)
        x = _mixer_core(x, mod_all, norm_w, q, kl, vl, a, p, w_out_b, wp_bd, ps, attn_sink, l,
                        cache=(ck, cv), prev=y)
        if l < DEPTH - 1:
            side = (_side_job(ffn_w1, (l + 1, 0), W1_SIDE_BLOCKS),
                    _side_job(ffn_w2, (l + 1, 0), W2_SIDE_BLOCKS),
                    _side_job(w_in, (l + 1,), WIO_SIDE_BLOCKS))
            xs, (w1, w2, w_in_b) = _ffn_half((x,), mod_all, norm_w, w1, w2, l, 1, side=side)
        else:
            xs, _ = _ffn_half((x,), mod_all, norm_w, w1, w2, l, 1, split_out=True)
    y_prompt = xs[0].reshape(N_CTX_SEQ, CTX_SEQ, D_MODEL)
    y_sample = xs[1].reshape(N_LAT_SEQ, LAT_SEQ, D_MODEL)
    cache_shape = (N_CTX_SEQ, DEPTH, CTX_SEQ, N_KV_HEADS, HEAD_DIM)
    return y_prompt, y_sample, caches[0].reshape(cache_shape), caches[1].reshape(cache_shape)
```

```python
import functools

import jax
import jax.numpy as jnp
from jax import lax
from jax.experimental import pallas as pl
from jax.experimental.pallas import tpu as pltpu

D_MODEL = 1024
N_CTX_SEQ = 32
CTX_SEQ = 256
DEPTH = 4
N_LAT_SEQ = 2
LAT_SEQ = 2048
PAST_LEN = 512
GRID_W = 64
HEAD_DIM = 64
ATTN_WIDTH = 512
N_HEADS = 8
N_KV_HEADS = 2
KV_WIDTH = 128
WINDOW = 128
GM_WIDTH = 256
GM_HEADS = 4
GM_DIM = 64
CHUNK = 128
POOL_WIDTH = 256
POOL_HALF_WINDOWS = (1, 2, 4, 8)
POOL_DIM = 64
IN_WIDTH = 1536
D_FF = 2816
N_MOD = 9
EPS = 1e-6
ROPE_BASE = 10000.0
NEG_INF = -1e30
ATTN_SCALE = HEAD_DIM ** -0.5

T_CTX = N_CTX_SEQ * CTX_SEQ
T_LAT = N_LAT_SEQ * LAT_SEQ
T_ALL = T_CTX + T_LAT
N_COND = 8

LANES = 128
VMEM_LIMIT = 56 * 1024 * 1024

FFN_TM = 512
FFN_CHUNKS = (512, 512, 512, 512, 512, 256)
W1_SIDE_BLOCKS = 16
W2_SIDE_BLOCKS = 22
WIO_SIDE_BLOCKS = 16
PROJ_TM = 1024
Q_BLK = 256
BAND = Q_BLK + 2 * WINDOW
CORE_CTX_ROWS = 1024
CORE_LAT_ROWS = 512
LOG2E = 1.4426950408889634
LAT_HEAD_BATCH = 4
CTX_HEAD_BATCH = 1
MOD_TN = 2304

BF16 = jnp.bfloat16
F32 = jnp.float32


def _dot(a, b):
    return jnp.dot(a, b, preferred_element_type=F32)


def _dot_t(a, b):
    return lax.dot_general(a, b, (((1,), (1,)), ((), ())), preferred_element_type=F32)


def _rms(x, g):
    return x * lax.rsqrt(jnp.mean(x * x, axis=-1, keepdims=True) + EPS) * g


def _zero_from(v):
    bits = lax.bitcast_convert_type(v, jnp.uint32)
    return ((bits >> 16) >> 16).astype(F32)


def _cond_row(tile, tm):
    n_ctx = T_CTX // tm
    per_seq = LAT_SEQ // tm
    return jnp.where(tile < n_ctx, 0, 1 + (tile - n_ctx) // per_seq)


def _mod_kernel(c_ref, w_ref, b_ref, o_ref):
    c = c_ref[...]
    s = jax.nn.silu(c).astype(BF16)
    o_ref[...] = _dot(s, w_ref[...].astype(BF16)) + b_ref[...]


def _adaln(cond, w_mod, b_mod):
    n_out = N_MOD * D_MODEL
    return pl.pallas_call(
        _mod_kernel,
        out_shape=jax.ShapeDtypeStruct((DEPTH, N_COND, n_out), F32),
        grid=(DEPTH, n_out // MOD_TN),
        in_specs=[
            pl.BlockSpec((N_COND, D_MODEL), lambda l, n: (0, 0)),
            pl.BlockSpec((None, D_MODEL, MOD_TN), lambda l, n: (l, 0, n)),
            pl.BlockSpec((None, 1, MOD_TN), lambda l, n: (l, 0, n)),
        ],
        out_specs=pl.BlockSpec((None, N_COND, MOD_TN), lambda l, n: (l, 0, n)),
        compiler_params=pltpu.CompilerParams(
            dimension_semantics=("arbitrary", "arbitrary"), vmem_limit_bytes=VMEM_LIMIT),
        name="adaln",
    )(cond, w_mod, b_mod.reshape(DEPTH, 1, n_out))


def _ffn_kernel(*refs, j, pre, post, split_in, split_out, n_side):
    n_x = 2 if split_in else 1
    n_o = 2 if split_out else 1
    xp_refs = refs[:n_x]
    xn_refs = refs[n_x:2 * n_x]
    modp_ref, modn_ref, nw_ref, w1_ref, w2_ref = refs[2 * n_x:2 * n_x + 5]
    pos = 2 * n_x + 5
    side_in = refs[pos:pos + n_side]
    o_refs = refs[pos + n_side:pos + n_side + n_o]
    side_out = refs[pos + n_side + n_o:pos + 2 * n_side + n_o]
    scratch = refs[pos + 2 * n_side + n_o:]
    h_refs, acc_ref = scratch[:2], scratch[2]
    i = pl.program_id(0)
    n_tiles = T_ALL // FFN_TM
    n_ctx = T_CTX // FFN_TM

    def load_x(x_refs, tile, rows):
        if split_in:
            return jnp.where(tile < n_ctx, x_refs[0][rows, :], x_refs[1][rows, :])
        return x_refs[0][rows, :]

    def prologue(x_refs, mod_ref, tile, rows):
        shift = mod_ref[0, j:j + 1, :]
        scale = mod_ref[0, j + 1:j + 2, :]
        x = load_x(x_refs, tile, rows)
        return _rms(x, nw_ref[pre:pre + 1, :]) * (1 + scale) + shift

    def epilogue(rows):
        gate = modp_ref[0, j + 2:j + 3, :]
        y = (load_x(xp_refs, i - 1, rows)
             + 0.5 * gate * _rms(acc_ref[rows, :], nw_ref[post:post + 1, :]))
        if split_out:
            scratch[3][rows, :] = y
        else:
            o_refs[0][rows, :] = y
        return y

    def store_split():
        prev_ctx = i - 1 < n_ctx

        @pl.when(prev_ctx)
        def _():
            o_refs[0][...] = scratch[3][...]

        @pl.when(jnp.logical_not(prev_ctx))
        def _():
            o_refs[1][...] = scratch[3][...]

    all_rows = slice(0, FFN_TM)
    halves = (slice(0, FFN_TM // 2), slice(FFN_TM // 2, FFN_TM))

    @pl.when(i == 0)
    def _():
        h_refs[0][...] = prologue(xp_refs, modp_ref, 0, all_rows).astype(BF16)
        acc_ref[...] = jnp.zeros_like(acc_ref)

    def main(h_cur, h_nxt):
        def vpu_piece(c):
            if c in (0, 1):
                return _zero_from(epilogue(halves[c])[:, 0:1])
            if c in (2, 3):
                rows = halves[c - 2]
                hf = prologue(xn_refs, modn_ref, i + 1, rows)
                h_nxt[rows, :] = hf.astype(BF16)
                return _zero_from(hf[:, 0:1])
            if c == 4:
                for s_in, s_out in zip(side_in, side_out):
                    s_out[...] = s_in[...].astype(BF16)
            return None

        acc = None
        off = 0
        tie = None
        for c, tk in enumerate(FFN_CHUNKS):
            g = _dot(h_cur[...], w1_ref[:, off:off + tk])
            u = _dot(h_cur[...], w1_ref[:, D_FF + off:D_FF + off + tk])
            if tie is not None:
                tied = halves[(c - 1) % 2]
                u = jnp.concatenate(
                    [u[r, :] + tie if r is tied else u[r, :] for r in halves], axis=0)
            a = (jax.nn.silu(g) * u).astype(BF16)
            part = _dot(a, w2_ref[off:off + tk, :])
            acc = part if acc is None else acc + part
            off += tk
            tie = vpu_piece(c)
        acc_ref[...] = acc

    for par in range(2):
        @pl.when(jnp.logical_and(i < n_tiles, lax.rem(i, 2) == par))
        def _(par=par):
            main(h_refs[par], h_refs[1 - par])

    @pl.when(i == n_tiles)
    def _():
        epilogue(all_rows)

    if split_out:
        store_split()


def _side_job(arr, lead, n_blk):
    rows, cols = arr.shape[-2:]
    blk = rows // n_blk
    nones = (None,) * len(lead)
    in_spec = pl.BlockSpec(nones + (blk, cols), lambda i: lead + (jnp.minimum(i, n_blk - 1), 0))
    out_spec = pl.BlockSpec((blk, cols), lambda i: (jnp.minimum(i, n_blk - 1), 0))
    return arr, in_spec, out_spec, jax.ShapeDtypeStruct((rows, cols), BF16)


def _ffn_half(xs, mod_all, norm_w, w1, w2, l, half, split_out=False, side=()):
    j, pre, post = (0, 0, 1) if half == 0 else (6, 4, 5)
    tm = FFN_TM
    n_tiles = T_ALL // tm
    n_ctx = T_CTX // tm
    n_lat = T_LAT // tm
    split_in = len(xs) == 2
    tile = (tm, D_MODEL)

    def specs(shift):
        t = lambda i: jnp.clip(i + shift, 0, n_tiles - 1)
        if split_in:
            return [pl.BlockSpec(tile, lambda i: (jnp.minimum(t(i), n_ctx - 1), 0)),
                    pl.BlockSpec(tile, lambda i: (jnp.clip(t(i) - n_ctx, 0, n_lat - 1), 0))]
        return [pl.BlockSpec(tile, lambda i: (t(i), 0))]

    def mod_spec(shift):
        t = lambda i: jnp.clip(i + shift, 0, n_tiles - 1)
        return pl.BlockSpec((None, 1, N_MOD, D_MODEL), lambda i: (l, _cond_row(t(i), tm), 0, 0))

    prev = lambda i: jnp.clip(i - 1, 0, n_tiles - 1)
    if split_out:
        out_shape = [jax.ShapeDtypeStruct((T_CTX, D_MODEL), F32),
                     jax.ShapeDtypeStruct((T_LAT, D_MODEL), F32)]
        out_specs = [pl.BlockSpec(tile, lambda i: (jnp.minimum(prev(i), n_ctx - 1), 0)),
                     pl.BlockSpec(tile, lambda i: (jnp.clip(prev(i) - n_ctx, 0, n_lat - 1), 0))]
    else:
        out_shape = [jax.ShapeDtypeStruct((T_ALL, D_MODEL), F32)]
        out_specs = [pl.BlockSpec(tile, lambda i: (prev(i), 0))]
    scratch = [pltpu.VMEM(tile, BF16), pltpu.VMEM(tile, BF16), pltpu.VMEM(tile, F32)]
    if split_out:
        scratch.append(pltpu.VMEM(tile, F32))
    whole = lambda i: (0, 0)
    outs = pl.pallas_call(
        functools.partial(_ffn_kernel, j=j, pre=pre, post=post, split_in=split_in,
                          split_out=split_out, n_side=len(side)),
        out_shape=out_shape + [s[3] for s in side],
        grid=(n_tiles + 1,),
        in_specs=specs(-1) + specs(1) + [
            mod_spec(-1),
            mod_spec(1),
            pl.BlockSpec((None, 6, D_MODEL), lambda i: (l, 0, 0)),
            pl.BlockSpec((D_MODEL, 2 * D_FF), whole, pipeline_mode=pl.Buffered(1)),
            pl.BlockSpec((D_FF, D_MODEL), whole, pipeline_mode=pl.Buffered(1)),
        ] + [s[1] for s in side],
        out_specs=out_specs + [s[2] for s in side],
        scratch_shapes=scratch,
        compiler_params=pltpu.CompilerParams(
            dimension_semantics=("arbitrary",), vmem_limit_bytes=VMEM_LIMIT),
        name="ffn_half",
    )(*xs, *xs, mod_all, mod_all, norm_w, w1, w2, *[s[0] for s in side])
    n_o = len(out_shape)
    return outs[:n_o], outs[n_o:]


def _swap16(x):
    w = x.shape[-1]
    lane = lax.broadcasted_iota(jnp.int32, x.shape, x.ndim - 1)
    first = (lane & 31) < 16
    return jnp.where(first, pltpu.roll(x, w - 16, axis=x.ndim - 1), pltpu.roll(x, 16, axis=x.ndim - 1))


def _rope(x, cos_t, sin_t):
    cols = [x[:, j:j + LANES] for j in range(0, x.shape[-1], LANES)]
    cols = [c * cos_t + _swap16(c) * sin_t for c in cols]
    return cols[0] if len(cols) == 1 else jnp.concatenate(cols, axis=-1)


def _proj_kernel(x_ref, mod_ref, nw_ref, w_ref, cos_ref, sin_ref, ws_ref, bz_ref, *rest):
    q_ref, kc_ref, vc_ref, kl_ref, vl_ref, a_ref, p_ref = rest[-7:]
    i = pl.program_id(0)
    tm = x_ref.shape[0]
    x = x_ref[...]
    shift = mod_ref[0, 3:4, :]
    scale = mod_ref[0, 4:5, :]
    h = (_rms(x, nw_ref[2:3, :]) * (1 + scale) + shift).astype(BF16)
    proj = _dot(h, w_ref[...])
    q = proj[:, 0:ATTN_WIDTH] * (ATTN_SCALE * LOG2E)
    k = proj[:, ATTN_WIDTH:ATTN_WIDTH + KV_WIDTH]
    v = proj[:, ATTN_WIDTH + KV_WIDTH:ATTN_WIDTH + 2 * KV_WIDTH]
    gu = proj[:, 768:1024]
    gv = proj[:, 1024:1280]
    p_ref[...] = proj[:, 1280:1536]

    is_lat = i >= T_CTX // tm

    @pl.when(is_lat)
    def _():
        cos_t = cos_ref[...]
        sin_t = sin_ref[...]
        q_ref[...] = _rope(q, cos_t, sin_t).astype(BF16)
        kl_ref[...] = _rope(k, cos_t, sin_t)
        vl_ref[...] = v

    @pl.when(jnp.logical_not(is_lat))
    def _():
        q_ref[...] = q.astype(BF16)
        for s in range(tm // CTX_SEQ):
            kc_ref[s] = k[s * CTX_SEQ:(s + 1) * CTX_SEQ, :]
            vc_ref[s] = v[s * CTX_SEQ:(s + 1) * CTX_SEQ, :]

    r = lax.broadcasted_iota(jnp.int32, (GM_WIDTH, GM_WIDTH), 0)
    c = lax.broadcasted_iota(jnp.int32, (GM_WIDTH, GM_WIDTH), 1)
    grp_mean = jnp.where((r >> 6) == (c >> 6), 1.0 / GM_DIM, 0.0).astype(BF16)
    sq = gv * gv
    sq_hi = sq.astype(BF16)
    sq_lo = (sq - sq_hi.astype(F32)).astype(BF16)
    ms = _dot(sq_hi, grp_mean) + _dot(sq_lo, grp_mean)
    vh = (gv * lax.rsqrt(ms + EPS)).astype(BF16)
    lane = lax.broadcasted_iota(jnp.int32, (CHUNK, GM_WIDTH), 1)
    ws = ws_ref[...]
    bz = bz_ref[...]
    for n in range(tm // CHUNK):
        vn = vh[n * CHUNK:(n + 1) * CHUNK, :]
        bd = jnp.concatenate(
            [jnp.where((lane >> 6) == hh, vn, jnp.zeros_like(vn)) for hh in range(GM_HEADS)], axis=0)
        z = _dot(ws, bd) + bz
        a_ref[n * CHUNK:(n + 1) * CHUNK, :] = (gu[n * CHUNK:(n + 1) * CHUNK, :] * z).astype(BF16)


def _mixer_proj(x, mod_all, norm_w, w_in, cos_t, sin_t, ws_cat, bz, l, caches=None):
    tm = PROJ_TM
    n_ctx = T_CTX // tm
    per_seq = LAT_SEQ // tm
    seq_per_tile = tm // CTX_SEQ

    def tbl_map(i):
        return (lax.rem(jnp.maximum(i - n_ctx, 0), per_seq), 0)

    row = lambda i: (i, 0)
    lat_row = lambda i: (jnp.maximum(i - n_ctx, 0), 0)
    cache_blk = pl.BlockSpec((seq_per_tile, None, CTX_SEQ, KV_WIDTH),
                             lambda i: (jnp.minimum(i, n_ctx - 1), l, 0, 0))
    cache_shape = jax.ShapeDtypeStruct((N_CTX_SEQ, DEPTH, CTX_SEQ, KV_WIDTH), F32)
    extra_specs, extra_args, aliases = [], [], {}
    if caches is not None:
        extra_specs = [pl.BlockSpec(memory_space=pl.ANY)] * 2
        extra_args = list(caches)
        aliases = {8: 1, 9: 2}
    return pl.pallas_call(
        _proj_kernel,
        out_shape=(
            jax.ShapeDtypeStruct((T_ALL, ATTN_WIDTH), BF16),
            cache_shape,
            cache_shape,
            jax.ShapeDtypeStruct((T_LAT, KV_WIDTH), F32),
            jax.ShapeDtypeStruct((T_LAT, KV_WIDTH), F32),
            jax.ShapeDtypeStruct((T_ALL, GM_WIDTH), BF16),
            jax.ShapeDtypeStruct((T_ALL, POOL_WIDTH), F32),
        ),
        grid=(T_ALL // tm,),
        input_output_aliases=aliases,
        in_specs=[
            pl.BlockSpec((tm, D_MODEL), row),
            pl.BlockSpec((None, 1, N_MOD, D_MODEL), lambda i: (l, _cond_row(i, tm), 0, 0)),
            pl.BlockSpec((None, 6, D_MODEL), lambda i: (l, 0, 0)),
            pl.BlockSpec((D_MODEL, IN_WIDTH), lambda i: (0, 0)),
            pl.BlockSpec((tm, LANES), tbl_map),
            pl.BlockSpec((tm, LANES), tbl_map),
            pl.BlockSpec((None, CHUNK, GM_HEADS * CHUNK), lambda i: (l, 0, 0)),
            pl.BlockSpec((None, CHUNK, GM_WIDTH), lambda i: (l, 0, 0)),
        ] + extra_specs,
        out_specs=(
            pl.BlockSpec((tm, ATTN_WIDTH), row),
            cache_blk,
            cache_blk,
            pl.BlockSpec((tm, KV_WIDTH), lat_row),
            pl.BlockSpec((tm, KV_WIDTH), lat_row),
            pl.BlockSpec((tm, GM_WIDTH), row),
            pl.BlockSpec((tm, POOL_WIDTH), row),
        ),
        compiler_params=pltpu.CompilerParams(
            dimension_semantics=("arbitrary",), vmem_limit_bytes=VMEM_LIMIT),
        name="mixer_proj",
    )(x, mod_all, norm_w, w_in, cos_t, sin_t, ws_cat, bz, *extra_args)


def _pool(x, wp, scale):
    s_len = x.shape[0]
    row = lax.broadcasted_iota(jnp.int32, x.shape, 0)
    lane = lax.broadcasted_iota(jnp.int32, x.shape, 1)

    def down(y, s):
        return jnp.where(row >= s, pltpu.roll(y, s, axis=0), 0.0)

    def up(y, s):
        return jnp.where(row < s_len - s, pltpu.roll(y, s_len - s, axis=0), 0.0)

    back = down(x, 1)
    fwd = x
    sums = [back + fwd]
    for a in POOL_HALF_WINDOWS[:-1]:
        back = back + down(back, a)
        fwd = fwd + up(fwd, a)
        sums.append(back + fwd)
    grp = lane >> 6
    tot = jnp.where(grp == 0, sums[0], jnp.where(grp == 1, sums[1], jnp.where(grp == 2, sums[2], sums[3])))
    half = jnp.where(grp == 0, POOL_HALF_WINDOWS[0],
                     jnp.where(grp == 1, POOL_HALF_WINDOWS[1],
                               jnp.where(grp == 2, POOL_HALF_WINDOWS[2], POOL_HALF_WINDOWS[3])))
    cnt = (jnp.minimum(row + half, s_len) - jnp.maximum(row - half, 0)).astype(F32)
    d = (tot / cnt - x).astype(BF16)
    return _dot(d, wp) * scale


def _dup_halves(t):
    lane = lax.broadcasted_iota(jnp.int32, t.shape, 1)
    lo = lane < HEAD_DIM
    sw = pltpu.roll(t, HEAD_DIM, axis=1)
    return (jnp.where(lo, t, sw).astype(BF16), jnp.where(lo, sw, t).astype(BF16))


def _head_values(v):
    lane = lax.broadcasted_iota(jnp.int32, v.shape, 1)
    lo = lane < HEAD_DIM
    sw = pltpu.roll(v, HEAD_DIM, axis=1)
    one = jnp.ones_like(v)
    pick = lambda a, b: jnp.where(lo, a, b).astype(BF16)
    return ((pick(v, one), pick(one, sw)), (pick(sw, one), pick(one, v)))


def _attend(q_ref, r0, segs, sink_ref, l, cat_ref, batch):
    lane = lax.broadcasted_iota(jnp.int32, (Q_BLK, LANES), 1)
    lo = lane < HEAD_DIM
    rows = slice(r0, r0 + Q_BLK)
    group_heads = [(p, hf) for p in range(2) for hf in range(2)]
    for g in range(N_KV_HEADS):
        outs, terms = {}, {}
        for b0 in range(0, len(group_heads), batch):
            heads = group_heads[b0:b0 + batch]
            sinks = [sink_ref[l, g * 4 + p * 2 + hf] * LOG2E for p, hf in heads]
            scores = []
            for p, hf in heads:
                col = (2 * g + p) * LANES
                qpair = q_ref[rows, col:col + LANES]
                sel = lo if hf == 0 else jnp.logical_not(lo)
                qm = jnp.where(sel, qpair, jnp.zeros_like(qpair))
                s_h = []
                for keys, _, bias in segs(g):
                    s = _dot_t(qm, keys)
                    s_h.append(s if bias is None else s + bias)
                scores.append(s_h)
            maxes = []
            for s_h, sink in zip(scores, sinks):
                m = jnp.full((Q_BLK, 1), sink, F32)
                for s in s_h:
                    m = jnp.maximum(m, jnp.max(s, axis=-1, keepdims=True))
                maxes.append(m)
            for (p, hf), s_h, m, sink in zip(heads, scores, maxes, sinks):
                o = None
                for s, (_, vals, _) in zip(s_h, segs(g)):
                    part = _dot(jnp.exp2(s - m).astype(BF16), vals[hf])
                    o = part if o is None else o + part
                outs[p, hf] = o
                terms[p, hf] = jnp.exp2(sink - m)
        for p in range(2):
            num = jnp.where(lo, outs[p, 0], outs[p, 1])
            den = jnp.where(lo, outs[p, 1] + terms[p, 1], outs[p, 0] + terms[p, 0])
            col = (2 * g + p) * LANES
            cat_ref[rows, col:col + LANES] = (num / pltpu.roll(den, HEAD_DIM, axis=1)).astype(BF16)


def _core_kernel(*refs, latent, l):
    if latent:
        (x_ref, mod_ref, nw_ref, q_ref, k_ref, v_ref, a_ref, p_ref, ck_ref, cv_ref,
         wo_ref, wp_ref, ps_ref, sink_ref, o_ref, cat_ref, pool_ref) = refs
    else:
        (x_ref, mod_ref, nw_ref, q_ref, k_ref, v_ref, a_ref, p_ref,
         wo_ref, wp_ref, ps_ref, sink_ref, o_ref, cat_ref) = refs
    n_rows = x_ref.shape[0]
    pool_cols = slice(ATTN_WIDTH + GM_WIDTH, D_MODEL)

    if latent:
        jq = pl.program_id(1)
        s_len = k_ref.shape[0]

        @pl.when(jq == 0)
        def _():
            pool_ref[...] = _pool(p_ref[...], wp_ref[...], ps_ref[...]).astype(BF16)

        ckk = _dup_halves(ck_ref[...])
        cvv = _head_values(cv_ref[...])
        for s in range(n_rows // Q_BLK):
            q0 = jq * n_rows + s * Q_BLK
            start = pl.multiple_of(jnp.clip(q0 - WINDOW, 0, s_len - BAND), WINDOW)
            kk = _dup_halves(k_ref[pl.ds(start, BAND), :])
            vv = _head_values(v_ref[pl.ds(start, BAND), :])
            qpos = q0 + lax.broadcasted_iota(jnp.int32, (Q_BLK, BAND), 0)
            kpos = start + lax.broadcasted_iota(jnp.int32, (Q_BLK, BAND), 1)
            bias = jnp.where(jnp.abs(qpos - kpos) <= WINDOW, 0.0, NEG_INF)
            _attend(q_ref, s * Q_BLK, lambda g: [(kk[g], vv[g], bias), (ckk[g], cvv[g], None)],
                    sink_ref, l, cat_ref, LAT_HEAD_BATCH)
        cat_ref[:, pool_cols] = pool_ref[pl.ds(pl.multiple_of(jq * n_rows, n_rows), n_rows), :]
    else:
        for s in range(n_rows // Q_BLK):
            rows = slice(s * Q_BLK, (s + 1) * Q_BLK)
            kk = _dup_halves(k_ref[s])
            vv = _head_values(v_ref[s])
            cat_ref[rows, pool_cols] = _pool(p_ref[rows, :], wp_ref[...], ps_ref[...]).astype(BF16)
            _attend(q_ref, s * Q_BLK, lambda g: [(kk[g], vv[g], None)], sink_ref, l, cat_ref,
                    CTX_HEAD_BATCH)

    cat_ref[:, ATTN_WIDTH:ATTN_WIDTH + GM_WIDTH] = a_ref[...]
    out = _dot(cat_ref[...], wo_ref[...])
    gate = mod_ref[0, 5:6, :]
    o_ref[...] = x_ref[...] + gate * _rms(out, nw_ref[3:4, :])


def _mixer_core(x, mod_all, norm_w, q, k, v, a, p, w_out, wp_bd, pool_scale, sink, l,
                cache=None):
    latent = cache is not None
    if latent:
        n_rows = CORE_LAT_ROWS
        grid = (N_LAT_SEQ, LAT_SEQ // n_rows)
        blk = lambda b, j: (T_CTX // n_rows + b * grid[1] + j, 0)
        cond = lambda b, j: (l, 1 + b, 0, 0)
        kv_spec = pl.BlockSpec((LAT_SEQ, KV_WIDTH), lambda b, j: (b, 0))
        p_spec = pl.BlockSpec((LAT_SEQ, POOL_WIDTH), lambda b, j: (T_CTX // LAT_SEQ + b, 0))
    else:
        n_rows = CORE_CTX_ROWS
        grid = (T_CTX // n_rows, 1)
        blk = lambda b, j: (b, 0)
        cond = lambda b, j: (l, 0, 0, 0)
        kv_spec = pl.BlockSpec((n_rows // CTX_SEQ, None, CTX_SEQ, KV_WIDTH),
                               lambda b, j: (b, l, 0, 0))
        p_spec = pl.BlockSpec((n_rows, POOL_WIDTH), blk)
    in_specs = [
        pl.BlockSpec((n_rows, D_MODEL), blk),
        pl.BlockSpec((None, 1, N_MOD, D_MODEL), cond),
        pl.BlockSpec((None, 6, D_MODEL), lambda b, j: (l, 0, 0)),
        pl.BlockSpec((n_rows, ATTN_WIDTH), blk),
        kv_spec,
        kv_spec,
        pl.BlockSpec((n_rows, GM_WIDTH), blk),
        p_spec,
    ]
    args = [x, mod_all, norm_w, q, k, v, a, p]
    if latent:
        ck, cv = cache
        in_specs += [
            pl.BlockSpec((None, None, PAST_LEN, KV_WIDTH), lambda b, j: (b, l, 0, 0)),
            pl.BlockSpec((None, None, PAST_LEN, KV_WIDTH), lambda b, j: (b, l, 0, 0)),
        ]
        args += [ck, cv]
    in_specs += [
        pl.BlockSpec((D_MODEL, D_MODEL), lambda b, j: (0, 0)),
        pl.BlockSpec((None, POOL_WIDTH, POOL_WIDTH), lambda b, j: (l, 0, 0)),
        pl.BlockSpec((None, 1, POOL_WIDTH), lambda b, j: (l, 0, 0)),
        pl.BlockSpec(memory_space=pltpu.SMEM),
    ]
    args += [w_out, wp_bd, pool_scale, sink]
    kern = functools.partial(_core_kernel, latent=latent, l=l)
    scratch = [pltpu.VMEM((n_rows, D_MODEL), BF16)]
    if latent:
        scratch.append(pltpu.VMEM((LAT_SEQ, POOL_WIDTH), BF16))
    return pl.pallas_call(
        kern,
        out_shape=jax.ShapeDtypeStruct((T_ALL, D_MODEL), F32),
        grid=grid,
        in_specs=in_specs,
        out_specs=pl.BlockSpec((n_rows, D_MODEL), blk),
        scratch_shapes=scratch,
        input_output_aliases={0: 0},
        compiler_params=pltpu.CompilerParams(
            dimension_semantics=("arbitrary", "arbitrary"), vmem_limit_bytes=VMEM_LIMIT),
        name="mixer_core_lat" if latent else "mixer_core_ctx",
    )(*args)


def _rope_tables():
    t = jnp.arange(LAT_SEQ, dtype=jnp.int32)
    row = (t // GRID_W).astype(F32)
    col = (t % GRID_W).astype(F32)
    nf = HEAD_DIM // 4
    inv = ROPE_BASE ** (-jnp.arange(nf, dtype=F32) / nf)
    ar = row[:, None] * inv
    ac = col[:, None] * inv
    cos64 = jnp.concatenate([jnp.cos(ar), jnp.cos(ar), jnp.cos(ac), jnp.cos(ac)], axis=-1)
    sin64 = jnp.concatenate([-jnp.sin(ar), jnp.sin(ar), -jnp.sin(ac), jnp.sin(ac)], axis=-1)
    return jnp.tile(cos64, (1, 2)), jnp.tile(sin64, (1, 2))


def kernel(x_prompt, x_sample, cache_k, cache_v, c, c_ctx, w_mod, b_mod, norm_w, w_in, w_out,
           attn_sink, w_spatial, b_spatial, w_pool, pool_scale, ffn_w1, ffn_w2):
    cond = jnp.concatenate(
        [c_ctx[None, :], c, jnp.zeros((N_COND - 1 - N_LAT_SEQ, D_MODEL), F32)], axis=0)
    mod_all = _adaln(cond, w_mod, b_mod).reshape(DEPTH, N_COND, N_MOD, D_MODEL)

    w1 = ffn_w1[0, 0].astype(BF16)
    w2 = ffn_w2[0, 0].astype(BF16)
    w_in_b = w_in[0].astype(BF16)
    ws_cat = w_spatial.transpose(0, 2, 1, 3).reshape(DEPTH, CHUNK, GM_HEADS * CHUNK).astype(BF16)
    bz = jnp.repeat(b_spatial.transpose(0, 2, 1), GM_DIM, axis=2)
    eye = jnp.eye(len(POOL_HALF_WINDOWS), dtype=F32)
    wp_bd = jnp.einsum('lgij,gh->lgihj', w_pool, eye).reshape(DEPTH, POOL_WIDTH, POOL_WIDTH).astype(BF16)
    ps = pool_scale.reshape(DEPTH, 1, POOL_WIDTH)
    cos_t, sin_t = _rope_tables()
    ck = cache_k.reshape(N_LAT_SEQ, DEPTH, PAST_LEN, KV_WIDTH)
    cv = cache_v.reshape(N_LAT_SEQ, DEPTH, PAST_LEN, KV_WIDTH)

    xs = (x_prompt.reshape(T_CTX, D_MODEL), x_sample.reshape(T_LAT, D_MODEL))
    caches = None
    for l in range(DEPTH):
        side = (_side_job(ffn_w1, (l, 1), W1_SIDE_BLOCKS), _side_job(ffn_w2, (l, 1), W2_SIDE_BLOCKS),
                _side_job(w_out, (l,), WIO_SIDE_BLOCKS))
        (x,), (w1, w2, w_out_b) = _ffn_half(xs, mod_all, norm_w, w1, w2, l, 0, side=side)
        q, kc, vc, kl, vl, a, p = _mixer_proj(
            x, mod_all, norm_w, w_in_b, cos_t, sin_t, ws_cat, bz, l, caches=caches)
        caches = (kc, vc)
        x = _mixer_core(x, mod_all, norm_w, q, kc, vc, a, p, w_out_b, wp_bd, ps, attn_sink, l)
        x = _mixer_core(x, mod_all, norm_w, q, kl, vl, a, p, w_out_b, wp_bd, ps, attn_sink, l,
                        cache=(ck, cv))
        if l < DEPTH - 1:
            side = (_side_job(ffn_w1, (l + 1, 0), W1_SIDE_BLOCKS),
                    _side_job(ffn_w2, (l + 1, 0), W2_SIDE_BLOCKS),
                    _side_job(w_in, (l + 1,), WIO_SIDE_BLOCKS))
            xs, (w1, w2, w_in_b) = _ffn_half((x,), mod_all, norm_w, w1, w2, l, 1, side=side)
        else:
            xs, _ = _ffn_half((x,), mod_all, norm_w, w1, w2, l, 1, split_out=True)
    y_prompt = xs[0].reshape(N_CTX_SEQ, CTX_SEQ, D_MODEL)
    y_sample = xs[1].reshape(N_LAT_SEQ, LAT_SEQ, D_MODEL)
    cache_shape = (N_CTX_SEQ, DEPTH, CTX_SEQ, N_KV_HEADS, HEAD_DIM)
    return y_prompt, y_sample, caches[0].reshape(cache_shape), caches[1].reshape(cache_shape)
```

```python
import functools

import jax
import jax.numpy as jnp
from jax import lax
from jax.experimental import pallas as pl
from jax.experimental.pallas import tpu as pltpu

D_MODEL = 1024
N_CTX_SEQ = 32
CTX_SEQ = 256
DEPTH = 4
N_LAT_SEQ = 2
LAT_SEQ = 2048
PAST_LEN = 512
GRID_W = 64
HEAD_DIM = 64
ATTN_WIDTH = 512
N_HEADS = 8
N_KV_HEADS = 2
KV_WIDTH = 128
WINDOW = 128
GM_WIDTH = 256
GM_HEADS = 4
GM_DIM = 64
CHUNK = 128
POOL_WIDTH = 256
POOL_HALF_WINDOWS = (1, 2, 4, 8)
POOL_DIM = 64
IN_WIDTH = 1536
D_FF = 2816
N_MOD = 9
EPS = 1e-6
ROPE_BASE = 10000.0
NEG_INF = -1e30
ATTN_SCALE = HEAD_DIM ** -0.5

T_CTX = N_CTX_SEQ * CTX_SEQ
T_LAT = N_LAT_SEQ * LAT_SEQ
T_ALL = T_CTX + T_LAT
N_COND = 8

LANES = 128
VMEM_LIMIT = 56 * 1024 * 1024

FFN_TM = 512
FFN_CHUNKS = (512, 512, 512, 512, 512, 256)
W1_SIDE_BLOCKS = 16
W2_SIDE_BLOCKS = 22
WIO_SIDE_BLOCKS = 16
PROJ_TM = 1024
Q_BLK = 256
BAND = Q_BLK + 2 * WINDOW
CORE_CTX_ROWS = 1024
CORE_LAT_ROWS = 512
LOG2E = 1.4426950408889634
LAT_HEAD_BATCH = 16
CTX_HEAD_BATCH = 16
MOD_TN = 2304

BF16 = jnp.bfloat16
F32 = jnp.float32


def _dot(a, b):
    return jnp.dot(a, b, preferred_element_type=F32)


def _dot_t(a, b):
    return lax.dot_general(a, b, (((1,), (1,)), ((), ())), preferred_element_type=F32)


def _rms(x, g):
    return x * lax.rsqrt(jnp.mean(x * x, axis=-1, keepdims=True) + EPS) * g


def _zero_from(v):
    bits = lax.bitcast_convert_type(v, jnp.uint32)
    return ((bits >> 16) >> 16).astype(F32)


def _cond_row(tile, tm):
    n_ctx = T_CTX // tm
    per_seq = LAT_SEQ // tm
    return jnp.where(tile < n_ctx, 0, 1 + (tile - n_ctx) // per_seq)


def _mod_kernel(c_ref, w_ref, b_ref, o_ref):
    c = c_ref[...]
    s = jax.nn.silu(c).astype(BF16)
    o_ref[...] = _dot(s, w_ref[...].astype(BF16)) + b_ref[...]


def _adaln(cond, w_mod, b_mod):
    n_out = N_MOD * D_MODEL
    return pl.pallas_call(
        _mod_kernel,
        out_shape=jax.ShapeDtypeStruct((DEPTH, N_COND, n_out), F32),
        grid=(DEPTH, n_out // MOD_TN),
        in_specs=[
            pl.BlockSpec((N_COND, D_MODEL), lambda l, n: (0, 0)),
            pl.BlockSpec((None, D_MODEL, MOD_TN), lambda l, n: (l, 0, n)),
            pl.BlockSpec((None, 1, MOD_TN), lambda l, n: (l, 0, n)),
        ],
        out_specs=pl.BlockSpec((None, N_COND, MOD_TN), lambda l, n: (l, 0, n)),
        compiler_params=pltpu.CompilerParams(
            dimension_semantics=("arbitrary", "arbitrary"), vmem_limit_bytes=VMEM_LIMIT),
        name="adaln",
    )(cond, w_mod, b_mod.reshape(DEPTH, 1, n_out))


def _ffn_kernel(*refs, j, pre, post, split_in, split_out, n_side):
    n_x = 2 if split_in else 1
    n_o = 2 if split_out else 1
    xp_refs = refs[:n_x]
    xn_refs = refs[n_x:2 * n_x]
    modp_ref, modn_ref, nw_ref, w1_ref, w2_ref = refs[2 * n_x:2 * n_x + 5]
    pos = 2 * n_x + 5
    side_in = refs[pos:pos + n_side]
    o_refs = refs[pos + n_side:pos + n_side + n_o]
    side_out = refs[pos + n_side + n_o:pos + 2 * n_side + n_o]
    scratch = refs[pos + 2 * n_side + n_o:]
    h_refs, acc_ref = scratch[:2], scratch[2]
    i = pl.program_id(0)
    n_tiles = T_ALL // FFN_TM
    n_ctx = T_CTX // FFN_TM

    def load_x(x_refs, tile, rows):
        if split_in:
            return jnp.where(tile < n_ctx, x_refs[0][rows, :], x_refs[1][rows, :])
        return x_refs[0][rows, :]

    def prologue(x_refs, mod_ref, tile, rows):
        shift = mod_ref[0, j:j + 1, :]
        scale = mod_ref[0, j + 1:j + 2, :]
        x = load_x(x_refs, tile, rows)
        return _rms(x, nw_ref[pre:pre + 1, :]) * (1 + scale) + shift

    def epilogue(rows):
        gate = modp_ref[0, j + 2:j + 3, :]
        y = (load_x(xp_refs, i - 1, rows)
             + 0.5 * gate * _rms(acc_ref[rows, :], nw_ref[post:post + 1, :]))
        if split_out:
            scratch[3][rows, :] = y
        else:
            o_refs[0][rows, :] = y
        return y

    def store_split():
        prev_ctx = i - 1 < n_ctx

        @pl.when(prev_ctx)
        def _():
            o_refs[0][...] = scratch[3][...]

        @pl.when(jnp.logical_not(prev_ctx))
        def _():
            o_refs[1][...] = scratch[3][...]

    all_rows = slice(0, FFN_TM)
    halves = (slice(0, FFN_TM // 2), slice(FFN_TM // 2, FFN_TM))

    @pl.when(i == 0)
    def _():
        h_refs[0][...] = prologue(xp_refs, modp_ref, 0, all_rows).astype(BF16)
        acc_ref[...] = jnp.zeros_like(acc_ref)

    def main(h_cur, h_nxt):
        def vpu_piece(c):
            if c in (0, 1):
                return _zero_from(epilogue(halves[c])[:, 0:1])
            if c in (2, 3):
                rows = halves[c - 2]
                hf = prologue(xn_refs, modn_ref, i + 1, rows)
                h_nxt[rows, :] = hf.astype(BF16)
                return _zero_from(hf[:, 0:1])
            if c == 4:
                for s_in, s_out in zip(side_in, side_out):
                    s_out[...] = s_in[...].astype(BF16)
            return None

        acc = None
        off = 0
        tie = None
        for c, tk in enumerate(FFN_CHUNKS):
            g = _dot(h_cur[...], w1_ref[:, off:off + tk])
            u = _dot(h_cur[...], w1_ref[:, D_FF + off:D_FF + off + tk])
            if tie is not None:
                tied = halves[(c - 1) % 2]
                u = jnp.concatenate(
                    [u[r, :] + tie if r is tied else u[r, :] for r in halves], axis=0)
            a = (jax.nn.silu(g) * u).astype(BF16)
            part = _dot(a, w2_ref[off:off + tk, :])
            acc = part if acc is None else acc + part
            off += tk
            tie = vpu_piece(c)
        acc_ref[...] = acc

    for par in range(2):
        @pl.when(jnp.logical_and(i < n_tiles, lax.rem(i, 2) == par))
        def _(par=par):
            main(h_refs[par], h_refs[1 - par])

    @pl.when(i == n_tiles)
    def _():
        epilogue(all_rows)

    if split_out:
        store_split()


def _side_job(arr, lead, n_blk):
    rows, cols = arr.shape[-2:]
    blk = rows // n_blk
    nones = (None,) * len(lead)
    in_spec = pl.BlockSpec(nones + (blk, cols), lambda i: lead + (jnp.minimum(i, n_blk - 1), 0))
    out_spec = pl.BlockSpec((blk, cols), lambda i: (jnp.minimum(i, n_blk - 1), 0))
    return arr, in_spec, out_spec, jax.ShapeDtypeStruct((rows, cols), BF16)


def _ffn_half(xs, mod_all, norm_w, w1, w2, l, half, split_out=False, side=()):
    j, pre, post = (0, 0, 1) if half == 0 else (6, 4, 5)
    tm = FFN_TM
    n_tiles = T_ALL // tm
    n_ctx = T_CTX // tm
    n_lat = T_LAT // tm
    split_in = len(xs) == 2
    tile = (tm, D_MODEL)

    def specs(shift):
        t = lambda i: jnp.clip(i + shift, 0, n_tiles - 1)
        if split_in:
            return [pl.BlockSpec(tile, lambda i: (jnp.minimum(t(i), n_ctx - 1), 0)),
                    pl.BlockSpec(tile, lambda i: (jnp.clip(t(i) - n_ctx, 0, n_lat - 1), 0))]
        return [pl.BlockSpec(tile, lambda i: (t(i), 0))]

    def mod_spec(shift):
        t = lambda i: jnp.clip(i + shift, 0, n_tiles - 1)
        return pl.BlockSpec((None, 1, N_MOD, D_MODEL), lambda i: (l, _cond_row(t(i), tm), 0, 0))

    prev = lambda i: jnp.clip(i - 1, 0, n_tiles - 1)
    if split_out:
        out_shape = [jax.ShapeDtypeStruct((T_CTX, D_MODEL), F32),
                     jax.ShapeDtypeStruct((T_LAT, D_MODEL), F32)]
        out_specs = [pl.BlockSpec(tile, lambda i: (jnp.minimum(prev(i), n_ctx - 1), 0)),
                     pl.BlockSpec(tile, lambda i: (jnp.clip(prev(i) - n_ctx, 0, n_lat - 1), 0))]
    else:
        out_shape = [jax.ShapeDtypeStruct((T_ALL, D_MODEL), F32)]
        out_specs = [pl.BlockSpec(tile, lambda i: (prev(i), 0))]
    scratch = [pltpu.VMEM(tile, BF16), pltpu.VMEM(tile, BF16), pltpu.VMEM(tile, F32)]
    if split_out:
        scratch.append(pltpu.VMEM(tile, F32))
    whole = lambda i: (0, 0)
    outs = pl.pallas_call(
        functools.partial(_ffn_kernel, j=j, pre=pre, post=post, split_in=split_in,
                          split_out=split_out, n_side=len(side)),
        out_shape=out_shape + [s[3] for s in side],
        grid=(n_tiles + 1,),
        in_specs=specs(-1) + specs(1) + [
            mod_spec(-1),
            mod_spec(1),
            pl.BlockSpec((None, 6, D_MODEL), lambda i: (l, 0, 0)),
            pl.BlockSpec((D_MODEL, 2 * D_FF), whole, pipeline_mode=pl.Buffered(1)),
            pl.BlockSpec((D_FF, D_MODEL), whole, pipeline_mode=pl.Buffered(1)),
        ] + [s[1] for s in side],
        out_specs=out_specs + [s[2] for s in side],
        scratch_shapes=scratch,
        compiler_params=pltpu.CompilerParams(
            dimension_semantics=("arbitrary",), vmem_limit_bytes=VMEM_LIMIT),
        name="ffn_half",
    )(*xs, *xs, mod_all, mod_all, norm_w, w1, w2, *[s[0] for s in side])
    n_o = len(out_shape)
    return outs[:n_o], outs[n_o:]


def _swap16(x):
    w = x.shape[-1]
    lane = lax.broadcasted_iota(jnp.int32, x.shape, x.ndim - 1)
    first = (lane & 31) < 16
    return jnp.where(first, pltpu.roll(x, w - 16, axis=x.ndim - 1), pltpu.roll(x, 16, axis=x.ndim - 1))


def _rope(x, cos_t, sin_t):
    cols = [x[:, j:j + LANES] for j in range(0, x.shape[-1], LANES)]
    cols = [c * cos_t + _swap16(c) * sin_t for c in cols]
    return cols[0] if len(cols) == 1 else jnp.concatenate(cols, axis=-1)


def _proj_kernel(x_ref, mod_ref, nw_ref, w_ref, cos_ref, sin_ref, ws_ref, bz_ref, *rest):
    q_ref, kc_ref, vc_ref, kl_ref, vl_ref, a_ref, p_ref = rest[-7:]
    i = pl.program_id(0)
    tm = x_ref.shape[0]
    half = tm // 2
    shift = mod_ref[0, 3:4, :]
    scale = mod_ref[0, 4:5, :]
    r = lax.broadcasted_iota(jnp.int32, (GM_WIDTH, GM_WIDTH), 0)
    c = lax.broadcasted_iota(jnp.int32, (GM_WIDTH, GM_WIDTH), 1)
    grp_mean = jnp.where((r >> 6) == (c >> 6), 1.0 / GM_DIM, 0.0).astype(BF16)
    lane = lax.broadcasted_iota(jnp.int32, (CHUNK, GM_WIDTH), 1)
    ws = ws_ref[...]
    bz = bz_ref[...]
    mix_cols = ATTN_WIDTH + 2 * KV_WIDTH

    def pre(rows, after=None):
        x = x_ref[rows, :]
        if after is not None:
            x = x + after
        return _rms(x, nw_ref[2:3, :]) * (1 + scale) + shift

    def gmlp_pool(rows, gu, gv, pool_in):
        p_ref[rows, :] = pool_in
        sq = gv * gv
        sq_hi = sq.astype(BF16)
        sq_lo = (sq - sq_hi.astype(F32)).astype(BF16)
        ms = _dot(sq_hi, grp_mean) + _dot(sq_lo, grp_mean)
        vh = (gv * lax.rsqrt(ms + EPS)).astype(BF16)
        outs = []
        for n in range(half // CHUNK):
            vn = vh[n * CHUNK:(n + 1) * CHUNK, :]
            bd = jnp.concatenate(
                [jnp.where((lane >> 6) == hh, vn, jnp.zeros_like(vn)) for hh in range(GM_HEADS)],
                axis=0)
            outs.append(gu[n * CHUNK:(n + 1) * CHUNK, :] * (_dot(ws, bd) + bz))
        a = jnp.concatenate(outs, axis=0)
        a_ref[rows, :] = a.astype(BF16)
        return a

    rows_a, rows_b = slice(0, half), slice(half, tm)
    h_a = pre(rows_a)
    h_b = pre(rows_b, _zero_from(h_a[:, 0:1]))
    proj_a = _dot(h_a.astype(BF16), w_ref[...])
    a_a = gmlp_pool(rows_a, proj_a[:, 768:1024], proj_a[:, 1024:1280], proj_a[:, 1280:1536])
    qkv_b = _dot(h_b.astype(BF16), w_ref[:, 0:mix_cols])
    h_b2 = (h_b + _zero_from(a_a[:, 0:1])).astype(BF16)
    mix_b = _dot(h_b2, w_ref[:, mix_cols:])
    gmlp_pool(rows_b, mix_b[:, 0:256], mix_b[:, 256:512], mix_b[:, 512:768])

    q = [proj_a[:, 0:ATTN_WIDTH] * (ATTN_SCALE * LOG2E), qkv_b[:, 0:ATTN_WIDTH] * (ATTN_SCALE * LOG2E)]
    k = [proj_a[:, ATTN_WIDTH:ATTN_WIDTH + KV_WIDTH], qkv_b[:, ATTN_WIDTH:ATTN_WIDTH + KV_WIDTH]]
    v = [proj_a[:, ATTN_WIDTH + KV_WIDTH:mix_cols], qkv_b[:, ATTN_WIDTH + KV_WIDTH:mix_cols]]
    is_lat = i >= T_CTX // tm

    @pl.when(is_lat)
    def _():
        for n, rows in enumerate((rows_a, rows_b)):
            cos_t = cos_ref[rows, :]
            sin_t = sin_ref[rows, :]
            q_ref[rows, :] = _rope(q[n], cos_t, sin_t).astype(BF16)
            kl_ref[rows, :] = _rope(k[n], cos_t, sin_t)
            vl_ref[rows, :] = v[n]

    @pl.when(jnp.logical_not(is_lat))
    def _():
        for n, rows in enumerate((rows_a, rows_b)):
            q_ref[rows, :] = q[n].astype(BF16)
            for s in range(half // CTX_SEQ):
                seq = n * (half // CTX_SEQ) + s
                kc_ref[seq] = k[n][s * CTX_SEQ:(s + 1) * CTX_SEQ, :]
                vc_ref[seq] = v[n][s * CTX_SEQ:(s + 1) * CTX_SEQ, :]


def _mixer_proj(x, mod_all, norm_w, w_in, cos_t, sin_t, ws_cat, bz, l, caches=None):
    tm = PROJ_TM
    n_ctx = T_CTX // tm
    per_seq = LAT_SEQ // tm
    seq_per_tile = tm // CTX_SEQ

    def tbl_map(i):
        return (lax.rem(jnp.maximum(i - n_ctx, 0), per_seq), 0)

    row = lambda i: (i, 0)
    lat_row = lambda i: (jnp.maximum(i - n_ctx, 0), 0)
    cache_blk = pl.BlockSpec((seq_per_tile, None, CTX_SEQ, KV_WIDTH),
                             lambda i: (jnp.minimum(i, n_ctx - 1), l, 0, 0))
    cache_shape = jax.ShapeDtypeStruct((N_CTX_SEQ, DEPTH, CTX_SEQ, KV_WIDTH), F32)
    extra_specs, extra_args, aliases = [], [], {}
    if caches is not None:
        extra_specs = [pl.BlockSpec(memory_space=pl.ANY)] * 2
        extra_args = list(caches)
        aliases = {8: 1, 9: 2}
    return pl.pallas_call(
        _proj_kernel,
        out_shape=(
            jax.ShapeDtypeStruct((T_ALL, ATTN_WIDTH), BF16),
            cache_shape,
            cache_shape,
            jax.ShapeDtypeStruct((T_LAT, KV_WIDTH), F32),
            jax.ShapeDtypeStruct((T_LAT, KV_WIDTH), F32),
            jax.ShapeDtypeStruct((T_ALL, GM_WIDTH), BF16),
            jax.ShapeDtypeStruct((T_ALL, POOL_WIDTH), F32),
        ),
        grid=(T_ALL // tm,),
        input_output_aliases=aliases,
        in_specs=[
            pl.BlockSpec((tm, D_MODEL), row),
            pl.BlockSpec((None, 1, N_MOD, D_MODEL), lambda i: (l, _cond_row(i, tm), 0, 0)),
            pl.BlockSpec((None, 6, D_MODEL), lambda i: (l, 0, 0)),
            pl.BlockSpec((D_MODEL, IN_WIDTH), lambda i: (0, 0)),
            pl.BlockSpec((tm, LANES), tbl_map),
            pl.BlockSpec((tm, LANES), tbl_map),
            pl.BlockSpec((None, CHUNK, GM_HEADS * CHUNK), lambda i: (l, 0, 0)),
            pl.BlockSpec((None, CHUNK, GM_WIDTH), lambda i: (l, 0, 0)),
        ] + extra_specs,
        out_specs=(
            pl.BlockSpec((tm, ATTN_WIDTH), row),
            cache_blk,
            cache_blk,
            pl.BlockSpec((tm, KV_WIDTH), lat_row),
            pl.BlockSpec((tm, KV_WIDTH), lat_row),
            pl.BlockSpec((tm, GM_WIDTH), row),
            pl.BlockSpec((tm, POOL_WIDTH), row),
        ),
        compiler_params=pltpu.CompilerParams(
            dimension_semantics=("arbitrary",), vmem_limit_bytes=VMEM_LIMIT),
        name="mixer_proj",
    )(x, mod_all, norm_w, w_in, cos_t, sin_t, ws_cat, bz, *extra_args)


def _pool(x, wp, scale):
    s_len = x.shape[0]
    row = lax.broadcasted_iota(jnp.int32, x.shape, 0)
    lane = lax.broadcasted_iota(jnp.int32, x.shape, 1)

    def down(y, s):
        return jnp.where(row >= s, pltpu.roll(y, s, axis=0), 0.0)

    def up(y, s):
        return jnp.where(row < s_len - s, pltpu.roll(y, s_len - s, axis=0), 0.0)

    back = down(x, 1)
    fwd = x
    sums = [back + fwd]
    for a in POOL_HALF_WINDOWS[:-1]:
        back = back + down(back, a)
        fwd = fwd + up(fwd, a)
        sums.append(back + fwd)
    grp = lane >> 6
    tot = jnp.where(grp == 0, sums[0], jnp.where(grp == 1, sums[1], jnp.where(grp == 2, sums[2], sums[3])))
    half = jnp.where(grp == 0, POOL_HALF_WINDOWS[0],
                     jnp.where(grp == 1, POOL_HALF_WINDOWS[1],
                               jnp.where(grp == 2, POOL_HALF_WINDOWS[2], POOL_HALF_WINDOWS[3])))
    cnt = (jnp.minimum(row + half, s_len) - jnp.maximum(row - half, 0)).astype(F32)
    d = (tot / cnt - x).astype(BF16)
    return _dot(d, wp) * scale


def _dup_halves(t):
    lane = lax.broadcasted_iota(jnp.int32, t.shape, 1)
    lo = lane < HEAD_DIM
    sw = pltpu.roll(t, HEAD_DIM, axis=1)
    return (jnp.where(lo, t, sw).astype(BF16), jnp.where(lo, sw, t).astype(BF16))


def _swap_row_halves(t):
    half = t.shape[0] // 2
    return jnp.concatenate([t[half:], t[:half]], axis=0)


def _head_values(v):
    vt = v.T
    top = lax.broadcasted_iota(jnp.int32, vt.shape, 0) < HEAD_DIM
    sw = _swap_row_halves(vt)
    one = jnp.ones_like(vt)
    pick = lambda a, b: jnp.where(top, a, b).astype(BF16)
    return ((pick(vt, one), pick(one, sw)), (pick(sw, one), pick(one, vt)))


def _attend(q_ref, jobs, sink_ref, l, cat_ref, batch):
    lane = lax.broadcasted_iota(jnp.int32, (Q_BLK, LANES), 1)
    lo = lane < HEAD_DIM
    top = lax.broadcasted_iota(jnp.int32, (LANES, Q_BLK), 0) < HEAD_DIM
    units = [(j, g, p, hf) for g in range(N_KV_HEADS) for p in range(2)
             for j in range(len(jobs)) for hf in range(2)]
    outs, terms = {}, {}
    for b0 in range(0, len(units), batch):
        batch_units = units[b0:b0 + batch]
        sinks = [sink_ref[l, g * 4 + p * 2 + hf] * LOG2E for _, g, p, hf in batch_units]
        scores = []
        for j, g, p, hf in batch_units:
            r0, segs = jobs[j]
            col = (2 * g + p) * LANES
            qpair = q_ref[r0:r0 + Q_BLK, col:col + LANES]
            sel = lo if hf == 0 else jnp.logical_not(lo)
            qm = jnp.where(sel, qpair, jnp.zeros_like(qpair))
            s_h = []
            for keys, _, bias in segs(g):
                s = _dot_t(keys, qm)
                s_h.append(s if bias is None else s + bias)
            scores.append(s_h)
        maxes = []
        for s_h, sink in zip(scores, sinks):
            m = jnp.full((1, Q_BLK), sink, F32)
            for s in s_h:
                m = jnp.maximum(m, jnp.max(s, axis=0, keepdims=True))
            maxes.append(m)
        for (j, g, p, hf), s_h, m, sink in zip(batch_units, scores, maxes, sinks):
            o = None
            for s, (_, vals, _) in zip(s_h, jobs[j][1](g)):
                part = _dot(vals[hf], jnp.exp2(s - m).astype(BF16))
                o = part if o is None else o + part
            outs[j, g, p, hf] = o
            terms[j, g, p, hf] = jnp.exp2(sink - m)
        for j, g, p, hf in batch_units:
            if hf == 0 or (j, g, p, 0) not in outs:
                continue
            o0, o1 = outs.pop((j, g, p, 0)), outs.pop((j, g, p, 1))
            num = jnp.where(top, o0, o1)
            den = jnp.where(top, o1 + terms[j, g, p, 1], o0 + terms[j, g, p, 0])
            r0, col = jobs[j][0], (2 * g + p) * LANES
            cat_ref[r0:r0 + Q_BLK, col:col + LANES] = (num / _swap_row_halves(den)).T.astype(BF16)


def _core_kernel(*refs, latent, l):
    if latent:
        (x_ref, mod_ref, nw_ref, q_ref, k_ref, v_ref, a_ref, p_ref, ck_ref, cv_ref,
         wo_ref, wp_ref, ps_ref, sink_ref, o_ref, cat_ref, pool_ref) = refs
    else:
        (x_ref, mod_ref, nw_ref, q_ref, k_ref, v_ref, a_ref, p_ref,
         wo_ref, wp_ref, ps_ref, sink_ref, o_ref, cat_ref) = refs
    n_rows = x_ref.shape[0]
    pool_cols = slice(ATTN_WIDTH + GM_WIDTH, D_MODEL)

    if latent:
        jq = pl.program_id(1)
        s_len = k_ref.shape[0]

        @pl.when(jq == 0)
        def _():
            pool_ref[...] = _pool(p_ref[...], wp_ref[...], ps_ref[...]).astype(BF16)

        ckk = _dup_halves(ck_ref[...])
        cvv = _head_values(cv_ref[...])
        jobs = []
        for s in range(n_rows // Q_BLK):
            q0 = jq * n_rows + s * Q_BLK
            start = pl.multiple_of(jnp.clip(q0 - WINDOW, 0, s_len - BAND), WINDOW)
            kk = _dup_halves(k_ref[pl.ds(start, BAND), :])
            vv = _head_values(v_ref[pl.ds(start, BAND), :])
            kpos = start + lax.broadcasted_iota(jnp.int32, (BAND, Q_BLK), 0)
            qpos = q0 + lax.broadcasted_iota(jnp.int32, (BAND, Q_BLK), 1)
            bias = jnp.where(jnp.abs(qpos - kpos) <= WINDOW, 0.0, NEG_INF)
            jobs.append((s * Q_BLK, lambda g, kk=kk, vv=vv, bias=bias: [
                (kk[g], vv[g], bias), (ckk[g], cvv[g], None)]))
        _attend(q_ref, jobs, sink_ref, l, cat_ref, LAT_HEAD_BATCH)
        cat_ref[:, pool_cols] = pool_ref[pl.ds(pl.multiple_of(jq * n_rows, n_rows), n_rows), :]
    else:
        jobs = []
        for s in range(n_rows // Q_BLK):
            rows = slice(s * Q_BLK, (s + 1) * Q_BLK)
            kk = _dup_halves(k_ref[s])
            vv = _head_values(v_ref[s])
            cat_ref[rows, pool_cols] = _pool(p_ref[rows, :], wp_ref[...], ps_ref[...]).astype(BF16)
            jobs.append((s * Q_BLK, lambda g, kk=kk, vv=vv: [(kk[g], vv[g], None)]))
        _attend(q_ref, jobs, sink_ref, l, cat_ref, CTX_HEAD_BATCH)

    cat_ref[:, ATTN_WIDTH:ATTN_WIDTH + GM_WIDTH] = a_ref[...]
    out = _dot(cat_ref[...], wo_ref[...])
    gate = mod_ref[0, 5:6, :]
    o_ref[...] = x_ref[...] + gate * _rms(out, nw_ref[3:4, :])


def _mixer_core(x, mod_all, norm_w, q, k, v, a, p, w_out, wp_bd, pool_scale, sink, l,
                cache=None):
    latent = cache is not None
    if latent:
        n_rows = CORE_LAT_ROWS
        grid = (N_LAT_SEQ, LAT_SEQ // n_rows)
        blk = lambda b, j: (T_CTX // n_rows + b * grid[1] + j, 0)
        cond = lambda b, j: (l, 1 + b, 0, 0)
        kv_spec = pl.BlockSpec((LAT_SEQ, KV_WIDTH), lambda b, j: (b, 0))
        p_spec = pl.BlockSpec((LAT_SEQ, POOL_WIDTH), lambda b, j: (T_CTX // LAT_SEQ + b, 0))
    else:
        n_rows = CORE_CTX_ROWS
        grid = (T_CTX // n_rows, 1)
        blk = lambda b, j: (b, 0)
        cond = lambda b, j: (l, 0, 0, 0)
        kv_spec = pl.BlockSpec((n_rows // CTX_SEQ, None, CTX_SEQ, KV_WIDTH),
                               lambda b, j: (b, l, 0, 0))
        p_spec = pl.BlockSpec((n_rows, POOL_WIDTH), blk)
    in_specs = [
        pl.BlockSpec((n_rows, D_MODEL), blk),
        pl.BlockSpec((None, 1, N_MOD, D_MODEL), cond),
        pl.BlockSpec((None, 6, D_MODEL), lambda b, j: (l, 0, 0)),
        pl.BlockSpec((n_rows, ATTN_WIDTH), blk),
        kv_spec,
        kv_spec,
        pl.BlockSpec((n_rows, GM_WIDTH), blk),
        p_spec,
    ]
    args = [x, mod_all, norm_w, q, k, v, a, p]
    if latent:
        ck, cv = cache
        in_specs += [
            pl.BlockSpec((None, None, PAST_LEN, KV_WIDTH), lambda b, j: (b, l, 0, 0)),
            pl.BlockSpec((None, None, PAST_LEN, KV_WIDTH), lambda b, j: (b, l, 0, 0)),
        ]
        args += [ck, cv]
    in_specs += [
        pl.BlockSpec((D_MODEL, D_MODEL), lambda b, j: (0, 0)),
        pl.BlockSpec((None, POOL_WIDTH, POOL_WIDTH), lambda b, j: (l, 0, 0)),
        pl.BlockSpec((None, 1, POOL_WIDTH), lambda b, j: (l, 0, 0)),
        pl.BlockSpec(memory_space=pltpu.SMEM),
    ]
    args += [w_out, wp_bd, pool_scale, sink]
    kern = functools.partial(_core_kernel, latent=latent, l=l)
    scratch = [pltpu.VMEM((n_rows, D_MODEL), BF16)]
    if latent:
        scratch.append(pltpu.VMEM((LAT_SEQ, POOL_WIDTH), BF16))
    return pl.pallas_call(
        kern,
        out_shape=jax.ShapeDtypeStruct((T_ALL, D_MODEL), F32),
        grid=grid,
        in_specs=in_specs,
        out_specs=pl.BlockSpec((n_rows, D_MODEL), blk),
        scratch_shapes=scratch,
        input_output_aliases={0: 0},
        compiler_params=pltpu.CompilerParams(
            dimension_semantics=("arbitrary", "arbitrary"), vmem_limit_bytes=VMEM_LIMIT),
        name="mixer_core_lat" if latent else "mixer_core_ctx",
    )(*args)


def _rope_tables():
    t = jnp.arange(LAT_SEQ, dtype=jnp.int32)
    row = (t // GRID_W).astype(F32)
    col = (t % GRID_W).astype(F32)
    nf = HEAD_DIM // 4
    inv = ROPE_BASE ** (-jnp.arange(nf, dtype=F32) / nf)
    ar = row[:, None] * inv
    ac = col[:, None] * inv
    cos64 = jnp.concatenate([jnp.cos(ar), jnp.cos(ar), jnp.cos(ac), jnp.cos(ac)], axis=-1)
    sin64 = jnp.concatenate([-jnp.sin(ar), jnp.sin(ar), -jnp.sin(ac), jnp.sin(ac)], axis=-1)
    return jnp.tile(cos64, (1, 2)), jnp.tile(sin64, (1, 2))


def kernel(x_prompt, x_sample, cache_k, cache_v, c, c_ctx, w_mod, b_mod, norm_w, w_in, w_out,
           attn_sink, w_spatial, b_spatial, w_pool, pool_scale, ffn_w1, ffn_w2):
    cond = jnp.concatenate(
        [c_ctx[None, :], c, jnp.zeros((N_COND - 1 - N_LAT_SEQ, D_MODEL), F32)], axis=0)
    mod_all = _adaln(cond, w_mod, b_mod).reshape(DEPTH, N_COND, N_MOD, D_MODEL)

    w1 = ffn_w1[0, 0].astype(BF16)
    w2 = ffn_w2[0, 0].astype(BF16)
    w_in_b = w_in[0].astype(BF16)
    ws_cat = w_spatial.transpose(0, 2, 1, 3).reshape(DEPTH, CHUNK, GM_HEADS * CHUNK).astype(BF16)
    bz = jnp.repeat(b_spatial.transpose(0, 2, 1), GM_DIM, axis=2)
    eye = jnp.eye(len(POOL_HALF_WINDOWS), dtype=F32)
    wp_bd = jnp.einsum('lgij,gh->lgihj', w_pool, eye).reshape(DEPTH, POOL_WIDTH, POOL_WIDTH).astype(BF16)
    ps = pool_scale.reshape(DEPTH, 1, POOL_WIDTH)
    cos_t, sin_t = _rope_tables()
    ck = cache_k.reshape(N_LAT_SEQ, DEPTH, PAST_LEN, KV_WIDTH)
    cv = cache_v.reshape(N_LAT_SEQ, DEPTH, PAST_LEN, KV_WIDTH)

    xs = (x_prompt.reshape(T_CTX, D_MODEL), x_sample.reshape(T_LAT, D_MODEL))
    caches = None
    for l in range(DEPTH):
        side = (_side_job(ffn_w1, (l, 1), W1_SIDE_BLOCKS), _side_job(ffn_w2, (l, 1), W2_SIDE_BLOCKS),
                _side_job(w_out, (l,), WIO_SIDE_BLOCKS))
        (x,), (w1, w2, w_out_b) = _ffn_half(xs, mod_all, norm_w, w1, w2, l, 0, side=side)
        q, kc, vc, kl, vl, a, p = _mixer_proj(
            x, mod_all, norm_w, w_in_b, cos_t, sin_t, ws_cat, bz, l, caches=caches)
        caches = (kc, vc)
        x = _mixer_core(x, mod_all, norm_w, q, kc, vc, a, p, w_out_b, wp_bd, ps, attn_sink, l)
        x = _mixer_core(x, mod_all, norm_w, q, kl, vl, a, p, w_out_b, wp_bd, ps, attn_sink, l,
                        cache=(ck, cv))
        if l < DEPTH - 1:
            side = (_side_job(ffn_w1, (l + 1, 0), W1_SIDE_BLOCKS),
                    _side_job(ffn_w2, (l + 1, 0), W2_SIDE_BLOCKS),
                    _side_job(w_in, (l + 1,), WIO_SIDE_BLOCKS))
            xs, (w1, w2, w_in_b) = _ffn_half((x,), mod_all, norm_w, w1, w2, l, 1, side=side)
        else:
            xs, _ = _ffn_half((x,), mod_all, norm_w, w1, w2, l, 1, split_out=True)
    y_prompt = xs[0].reshape(N_CTX_SEQ, CTX_SEQ, D_MODEL)
    y_sample = xs[1].reshape(N_LAT_SEQ, LAT_SEQ, D_MODEL)
    cache_shape = (N_CTX_SEQ, DEPTH, CTX_SEQ, N_KV_HEADS, HEAD_DIM)
    return y_prompt, y_sample, caches[0].reshape(cache_shape), caches[1].reshape(cache_shape)
```

```python
import functools

import jax
import jax.numpy as jnp
from jax import lax
from jax.experimental import pallas as pl
from jax.experimental.pallas import tpu as pltpu

D_MODEL = 1024
N_CTX_SEQ = 32
CTX_SEQ = 256
DEPTH = 4
N_LAT_SEQ = 2
LAT_SEQ = 2048
PAST_LEN = 512
GRID_W = 64
HEAD_DIM = 64
ATTN_WIDTH = 512
N_HEADS = 8
N_KV_HEADS = 2
KV_WIDTH = 128
WINDOW = 128
GM_WIDTH = 256
GM_HEADS = 4
GM_DIM = 64
CHUNK = 128
POOL_WIDTH = 256
POOL_HALF_WINDOWS = (1, 2, 4, 8)
POOL_DIM = 64
IN_WIDTH = 1536
D_FF = 2816
N_MOD = 9
EPS = 1e-6
ROPE_BASE = 10000.0
NEG_INF = -1e30
ATTN_SCALE = HEAD_DIM ** -0.5

T_CTX = N_CTX_SEQ * CTX_SEQ
T_LAT = N_LAT_SEQ * LAT_SEQ
T_ALL = T_CTX + T_LAT
N_COND = 8

LANES = 128
VMEM_LIMIT = 56 * 1024 * 1024

FFN_TM = 512
FFN_CHUNKS = (512, 512, 512, 512, 512, 256)
W1_SIDE_BLOCKS = 16
W2_SIDE_BLOCKS = 22
WIO_SIDE_BLOCKS = 16
PROJ_TM = 1024
Q_BLK = 256
BAND = Q_BLK + 2 * WINDOW
CORE_CTX_ROWS = 1024
CORE_LAT_ROWS = 512
LOG2E = 1.4426950408889634
LAT_HEAD_BATCH = 16
CTX_HEAD_BATCH = 16
MOD_TN = 2304

BF16 = jnp.bfloat16
F32 = jnp.float32


def _dot(a, b):
    return jnp.dot(a, b, preferred_element_type=F32)


def _dot_t(a, b):
    return lax.dot_general(a, b, (((1,), (1,)), ((), ())), preferred_element_type=F32)


def _rms(x, g):
    return x * lax.rsqrt(jnp.mean(x * x, axis=-1, keepdims=True) + EPS) * g


def _zero_from(v):
    bits = lax.bitcast_convert_type(v, jnp.uint32)
    return ((bits >> 16) >> 16).astype(F32)


def _cond_row(tile, tm):
    n_ctx = T_CTX // tm
    per_seq = LAT_SEQ // tm
    return jnp.where(tile < n_ctx, 0, 1 + (tile - n_ctx) // per_seq)


def _mod_kernel(c_ref, w_ref, b_ref, o_ref):
    c = c_ref[...]
    s = jax.nn.silu(c).astype(BF16)
    o_ref[...] = _dot(s, w_ref[...].astype(BF16)) + b_ref[...]


def _adaln(cond, w_mod, b_mod):
    n_out = N_MOD * D_MODEL
    return pl.pallas_call(
        _mod_kernel,
        out_shape=jax.ShapeDtypeStruct((DEPTH, N_COND, n_out), F32),
        grid=(DEPTH, n_out // MOD_TN),
        in_specs=[
            pl.BlockSpec((N_COND, D_MODEL), lambda l, n: (0, 0)),
            pl.BlockSpec((None, D_MODEL, MOD_TN), lambda l, n: (l, 0, n)),
            pl.BlockSpec((None, 1, MOD_TN), lambda l, n: (l, 0, n)),
        ],
        out_specs=pl.BlockSpec((None, N_COND, MOD_TN), lambda l, n: (l, 0, n)),
        compiler_params=pltpu.CompilerParams(
            dimension_semantics=("arbitrary", "arbitrary"), vmem_limit_bytes=VMEM_LIMIT),
        name="adaln",
    )(cond, w_mod, b_mod.reshape(DEPTH, 1, n_out))


def _ffn_kernel(*refs, j, pre, post, split_in, split_out, n_side):
    n_x = 2 if split_in else 1
    n_o = 2 if split_out else 1
    xn_refs = refs[:n_x]
    modp_ref, modn_ref, nw_ref, w1_ref, w2_ref = refs[n_x:n_x + 5]
    pos = n_x + 5
    side_in = refs[pos:pos + n_side]
    o_refs = refs[pos + n_side:pos + n_side + n_o]
    side_out = refs[pos + n_side + n_o:pos + 2 * n_side + n_o]
    scratch = refs[pos + 2 * n_side + n_o:]
    h_refs, x_refs, acc_ref = scratch[0:2], scratch[2:4], scratch[4]
    i = pl.program_id(0)
    n_tiles = T_ALL // FFN_TM
    n_ctx = T_CTX // FFN_TM

    def prologue(x_keep, rows):
        shift = modn_ref[0, j:j + 1, :]
        scale = modn_ref[0, j + 1:j + 2, :]
        if split_in:
            x = jnp.where(i < n_ctx, xn_refs[0][rows, :], xn_refs[1][rows, :])
        else:
            x = xn_refs[0][rows, :]
        x_keep[rows, :] = x
        return _rms(x, nw_ref[pre:pre + 1, :]) * (1 + scale) + shift

    def epilogue(x_kept, rows):
        gate = modp_ref[0, j + 2:j + 3, :]
        y = x_kept[rows, :] + 0.5 * gate * _rms(acc_ref[rows, :], nw_ref[post:post + 1, :])
        if split_out:
            scratch[5][rows, :] = y
        else:
            o_refs[0][rows, :] = y
        return y

    def store_split():
        prev_ctx = i - 2 < n_ctx

        @pl.when(prev_ctx)
        def _():
            o_refs[0][...] = scratch[5][...]

        @pl.when(jnp.logical_not(prev_ctx))
        def _():
            o_refs[1][...] = scratch[5][...]

    all_rows = slice(0, FFN_TM)
    halves = (slice(0, FFN_TM // 2), slice(FFN_TM // 2, FFN_TM))

    @pl.when(i == 0)
    def _():
        h_refs[0][...] = prologue(x_refs[0], all_rows).astype(BF16)
        acc_ref[...] = jnp.zeros_like(acc_ref)
        x_refs[1][...] = jnp.zeros_like(x_refs[1])

    def main(par):
        h_cur, h_nxt, x_other = h_refs[par], h_refs[1 - par], x_refs[1 - par]

        def vpu_piece(c):
            if c in (0, 1):
                return _zero_from(epilogue(x_other, halves[c])[:, 0:1])
            if c in (2, 3):
                rows = halves[c - 2]
                hf = prologue(x_other, rows)
                h_nxt[rows, :] = hf.astype(BF16)
                return _zero_from(hf[:, 0:1])
            if c == 4:
                for s_in, s_out in zip(side_in, side_out):
                    s_out[...] = s_in[...].astype(BF16)
            return None

        acc = None
        off = 0
        tie = None
        for c, tk in enumerate(FFN_CHUNKS):
            g = _dot(h_cur[...], w1_ref[:, off:off + tk])
            u = _dot(h_cur[...], w1_ref[:, D_FF + off:D_FF + off + tk])
            if tie is not None:
                tied = halves[(c - 1) % 2]
                u = jnp.concatenate(
                    [u[r, :] + tie if r is tied else u[r, :] for r in halves], axis=0)
            a = (jax.nn.silu(g) * u).astype(BF16)
            part = _dot(a, w2_ref[off:off + tk, :])
            acc = part if acc is None else acc + part
            off += tk
            tie = vpu_piece(c)
        acc_ref[...] = acc

    for par in range(2):
        @pl.when(jnp.logical_and(jnp.logical_and(i >= 1, i <= n_tiles), lax.rem(i + 1, 2) == par))
        def _(par=par):
            main(par)

    @pl.when(i == n_tiles + 1)
    def _():
        epilogue(x_refs[(n_tiles - 1) % 2], all_rows)

    if split_out:
        store_split()


def _side_job(arr, lead, n_blk):
    rows, cols = arr.shape[-2:]
    blk = rows // n_blk
    nones = (None,) * len(lead)
    blk_of = lambda i: jnp.clip(i - 1, 0, n_blk - 1)
    in_spec = pl.BlockSpec(nones + (blk, cols), lambda i: lead + (blk_of(i), 0))
    out_spec = pl.BlockSpec((blk, cols), lambda i: (blk_of(i), 0))
    return arr, in_spec, out_spec, jax.ShapeDtypeStruct((rows, cols), BF16)


def _ffn_half(xs, mod_all, norm_w, w1, w2, l, half, split_out=False, side=()):
    j, pre, post = (0, 0, 1) if half == 0 else (6, 4, 5)
    tm = FFN_TM
    n_tiles = T_ALL // tm
    n_ctx = T_CTX // tm
    n_lat = T_LAT // tm
    split_in = len(xs) == 2
    tile = (tm, D_MODEL)

    def specs(shift):
        t = lambda i: jnp.clip(i + shift, 0, n_tiles - 1)
        if split_in:
            return [pl.BlockSpec(tile, lambda i: (jnp.minimum(t(i), n_ctx - 1), 0)),
                    pl.BlockSpec(tile, lambda i: (jnp.clip(t(i) - n_ctx, 0, n_lat - 1), 0))]
        return [pl.BlockSpec(tile, lambda i: (t(i), 0))]

    def mod_spec(shift):
        t = lambda i: jnp.clip(i + shift, 0, n_tiles - 1)
        return pl.BlockSpec((None, 1, N_MOD, D_MODEL), lambda i: (l, _cond_row(t(i), tm), 0, 0))

    prev = lambda i: jnp.clip(i - 2, 0, n_tiles - 1)
    if split_out:
        out_shape = [jax.ShapeDtypeStruct((T_CTX, D_MODEL), F32),
                     jax.ShapeDtypeStruct((T_LAT, D_MODEL), F32)]
        out_specs = [pl.BlockSpec(tile, lambda i: (jnp.minimum(prev(i), n_ctx - 1), 0)),
                     pl.BlockSpec(tile, lambda i: (jnp.clip(prev(i) - n_ctx, 0, n_lat - 1), 0))]
    else:
        out_shape = [jax.ShapeDtypeStruct((T_ALL, D_MODEL), F32)]
        out_specs = [pl.BlockSpec(tile, lambda i: (prev(i), 0))]
    scratch = [pltpu.VMEM(tile, BF16), pltpu.VMEM(tile, BF16),
               pltpu.VMEM(tile, F32), pltpu.VMEM(tile, F32), pltpu.VMEM(tile, F32)]
    if split_out:
        scratch.append(pltpu.VMEM(tile, F32))
    whole = lambda i: (0, 0)
    outs = pl.pallas_call(
        functools.partial(_ffn_kernel, j=j, pre=pre, post=post, split_in=split_in,
                          split_out=split_out, n_side=len(side)),
        out_shape=out_shape + [s[3] for s in side],
        grid=(n_tiles + 2,),
        in_specs=specs(0) + [
            mod_spec(-2),
            mod_spec(0),
            pl.BlockSpec((None, 6, D_MODEL), lambda i: (l, 0, 0)),
            pl.BlockSpec((D_MODEL, 2 * D_FF), whole, pipeline_mode=pl.Buffered(1)),
            pl.BlockSpec((D_FF, D_MODEL), whole, pipeline_mode=pl.Buffered(1)),
        ] + [s[1] for s in side],
        out_specs=out_specs + [s[2] for s in side],
        scratch_shapes=scratch,
        compiler_params=pltpu.CompilerParams(
            dimension_semantics=("arbitrary",), vmem_limit_bytes=VMEM_LIMIT),
        name="ffn_half",
    )(*xs, mod_all, mod_all, norm_w, w1, w2, *[s[0] for s in side])
    n_o = len(out_shape)
    return outs[:n_o], outs[n_o:]


def _swap16(x):
    w = x.shape[-1]
    lane = lax.broadcasted_iota(jnp.int32, x.shape, x.ndim - 1)
    first = (lane & 31) < 16
    return jnp.where(first, pltpu.roll(x, w - 16, axis=x.ndim - 1), pltpu.roll(x, 16, axis=x.ndim - 1))


def _rope(x, cos_t, sin_t):
    cols = [x[:, j:j + LANES] for j in range(0, x.shape[-1], LANES)]
    cols = [c * cos_t + _swap16(c) * sin_t for c in cols]
    return cols[0] if len(cols) == 1 else jnp.concatenate(cols, axis=-1)


def _proj_kernel(x_ref, mod_ref, nw_ref, w_ref, cos_ref, sin_ref, ws_ref, bz_ref, *rest):
    q_ref, kc_ref, vc_ref, kl_ref, vl_ref, a_ref, p_ref = rest[-7:]
    i = pl.program_id(0)
    tm = x_ref.shape[0]
    half = tm // 2
    shift = mod_ref[0, 3:4, :]
    scale = mod_ref[0, 4:5, :]
    r = lax.broadcasted_iota(jnp.int32, (GM_WIDTH, GM_WIDTH), 0)
    c = lax.broadcasted_iota(jnp.int32, (GM_WIDTH, GM_WIDTH), 1)
    grp_mean = jnp.where((r >> 6) == (c >> 6), 1.0 / GM_DIM, 0.0).astype(BF16)
    lane = lax.broadcasted_iota(jnp.int32, (CHUNK, GM_WIDTH), 1)
    ws = ws_ref[...]
    bz = bz_ref[...]
    mix_cols = ATTN_WIDTH + 2 * KV_WIDTH

    def pre(rows, after=None):
        x = x_ref[rows, :]
        if after is not None:
            x = x + after
        return _rms(x, nw_ref[2:3, :]) * (1 + scale) + shift

    def gmlp_pool(rows, gu, gv, pool_in):
        p_ref[rows, :] = pool_in
        sq = gv * gv
        sq_hi = sq.astype(BF16)
        sq_lo = (sq - sq_hi.astype(F32)).astype(BF16)
        ms = _dot(sq_hi, grp_mean) + _dot(sq_lo, grp_mean)
        vh = (gv * lax.rsqrt(ms + EPS)).astype(BF16)
        outs = []
        for n in range(half // CHUNK):
            vn = vh[n * CHUNK:(n + 1) * CHUNK, :]
            bd = jnp.concatenate(
                [jnp.where((lane >> 6) == hh, vn, jnp.zeros_like(vn)) for hh in range(GM_HEADS)],
                axis=0)
            outs.append(gu[n * CHUNK:(n + 1) * CHUNK, :] * (_dot(ws, bd) + bz))
        a = jnp.concatenate(outs, axis=0)
        a_ref[rows, :] = a.astype(BF16)
        return a

    rows_a, rows_b = slice(0, half), slice(half, tm)
    h_a = pre(rows_a)
    h_b = pre(rows_b, _zero_from(h_a[:, 0:1]))
    proj_a = _dot(h_a.astype(BF16), w_ref[...])
    a_a = gmlp_pool(rows_a, proj_a[:, 768:1024], proj_a[:, 1024:1280], proj_a[:, 1280:1536])
    qkv_b = _dot(h_b.astype(BF16), w_ref[:, 0:mix_cols])
    h_b2 = (h_b + _zero_from(a_a[:, 0:1])).astype(BF16)
    mix_b = _dot(h_b2, w_ref[:, mix_cols:])
    gmlp_pool(rows_b, mix_b[:, 0:256], mix_b[:, 256:512], mix_b[:, 512:768])

    q = [proj_a[:, 0:ATTN_WIDTH] * (ATTN_SCALE * LOG2E), qkv_b[:, 0:ATTN_WIDTH] * (ATTN_SCALE * LOG2E)]
    k = [proj_a[:, ATTN_WIDTH:ATTN_WIDTH + KV_WIDTH], qkv_b[:, ATTN_WIDTH:ATTN_WIDTH + KV_WIDTH]]
    v = [proj_a[:, ATTN_WIDTH + KV_WIDTH:mix_cols], qkv_b[:, ATTN_WIDTH + KV_WIDTH:mix_cols]]
    is_lat = i >= T_CTX // tm

    @pl.when(is_lat)
    def _():
        for n, rows in enumerate((rows_a, rows_b)):
            cos_t = cos_ref[rows, :]
            sin_t = sin_ref[rows, :]
            q_ref[rows, :] = _rope(q[n], cos_t, sin_t).astype(BF16)
            kl_ref[rows, :] = _rope(k[n], cos_t, sin_t)
            vl_ref[rows, :] = v[n]

    @pl.when(jnp.logical_not(is_lat))
    def _():
        for n, rows in enumerate((rows_a, rows_b)):
            q_ref[rows, :] = q[n].astype(BF16)
            for s in range(half // CTX_SEQ):
                seq = n * (half // CTX_SEQ) + s
                kc_ref[seq] = k[n][s * CTX_SEQ:(s + 1) * CTX_SEQ, :]
                vc_ref[seq] = v[n][s * CTX_SEQ:(s + 1) * CTX_SEQ, :]


def _mixer_proj(x, mod_all, norm_w, w_in, cos_t, sin_t, ws_cat, bz, l, caches=None):
    tm = PROJ_TM
    n_ctx = T_CTX // tm
    per_seq = LAT_SEQ // tm
    seq_per_tile = tm // CTX_SEQ

    def tbl_map(i):
        return (lax.rem(jnp.maximum(i - n_ctx, 0), per_seq), 0)

    row = lambda i: (i, 0)
    lat_row = lambda i: (jnp.maximum(i - n_ctx, 0), 0)
    cache_blk = pl.BlockSpec((seq_per_tile, None, CTX_SEQ, KV_WIDTH),
                             lambda i: (jnp.minimum(i, n_ctx - 1), l, 0, 0))
    cache_shape = jax.ShapeDtypeStruct((N_CTX_SEQ, DEPTH, CTX_SEQ, KV_WIDTH), F32)
    extra_specs, extra_args, aliases = [], [], {}
    if caches is not None:
        extra_specs = [pl.BlockSpec(memory_space=pl.ANY)] * 2
        extra_args = list(caches)
        aliases = {8: 1, 9: 2}
    return pl.pallas_call(
        _proj_kernel,
        out_shape=(
            jax.ShapeDtypeStruct((T_ALL, ATTN_WIDTH), BF16),
            cache_shape,
            cache_shape,
            jax.ShapeDtypeStruct((T_LAT, KV_WIDTH), F32),
            jax.ShapeDtypeStruct((T_LAT, KV_WIDTH), F32),
            jax.ShapeDtypeStruct((T_ALL, GM_WIDTH), BF16),
            jax.ShapeDtypeStruct((T_ALL, POOL_WIDTH), F32),
        ),
        grid=(T_ALL // tm,),
        input_output_aliases=aliases,
        in_specs=[
            pl.BlockSpec((tm, D_MODEL), row),
            pl.BlockSpec((None, 1, N_MOD, D_MODEL), lambda i: (l, _cond_row(i, tm), 0, 0)),
            pl.BlockSpec((None, 6, D_MODEL), lambda i: (l, 0, 0)),
            pl.BlockSpec((D_MODEL, IN_WIDTH), lambda i: (0, 0)),
            pl.BlockSpec((tm, LANES), tbl_map),
            pl.BlockSpec((tm, LANES), tbl_map),
            pl.BlockSpec((None, CHUNK, GM_HEADS * CHUNK), lambda i: (l, 0, 0)),
            pl.BlockSpec((None, CHUNK, GM_WIDTH), lambda i: (l, 0, 0)),
        ] + extra_specs,
        out_specs=(
            pl.BlockSpec((tm, ATTN_WIDTH), row),
            cache_blk,
            cache_blk,
            pl.BlockSpec((tm, KV_WIDTH), lat_row),
            pl.BlockSpec((tm, KV_WIDTH), lat_row),
            pl.BlockSpec((tm, GM_WIDTH), row),
            pl.BlockSpec((tm, POOL_WIDTH), row),
        ),
        compiler_params=pltpu.CompilerParams(
            dimension_semantics=("arbitrary",), vmem_limit_bytes=VMEM_LIMIT),
        name="mixer_proj",
    )(x, mod_all, norm_w, w_in, cos_t, sin_t, ws_cat, bz, *extra_args)


def _pool(x, wp, scale):
    s_len = x.shape[0]
    row = lax.broadcasted_iota(jnp.int32, x.shape, 0)
    lane = lax.broadcasted_iota(jnp.int32, x.shape, 1)

    def down(y, s):
        return jnp.where(row >= s, pltpu.roll(y, s, axis=0), 0.0)

    def up(y, s):
        return jnp.where(row < s_len - s, pltpu.roll(y, s_len - s, axis=0), 0.0)

    back = down(x, 1)
    fwd = x
    sums = [back + fwd]
    for a in POOL_HALF_WINDOWS[:-1]:
        back = back + down(back, a)
        fwd = fwd + up(fwd, a)
        sums.append(back + fwd)
    grp = lane >> 6
    tot = jnp.where(grp == 0, sums[0], jnp.where(grp == 1, sums[1], jnp.where(grp == 2, sums[2], sums[3])))
    half = jnp.where(grp == 0, POOL_HALF_WINDOWS[0],
                     jnp.where(grp == 1, POOL_HALF_WINDOWS[1],
                               jnp.where(grp == 2, POOL_HALF_WINDOWS[2], POOL_HALF_WINDOWS[3])))
    cnt = (jnp.minimum(row + half, s_len) - jnp.maximum(row - half, 0)).astype(F32)
    d = (tot / cnt - x).astype(BF16)
    return _dot(d, wp) * scale


def _dup_halves(t):
    lane = lax.broadcasted_iota(jnp.int32, t.shape, 1)
    lo = lane < HEAD_DIM
    sw = pltpu.roll(t, HEAD_DIM, axis=1)
    return (jnp.where(lo, t, sw).astype(BF16), jnp.where(lo, sw, t).astype(BF16))


def _swap_row_halves(t):
    half = t.shape[0] // 2
    return jnp.concatenate([t[half:], t[:half]], axis=0)


def _head_values(v):
    vt = v.T
    top = lax.broadcasted_iota(jnp.int32, vt.shape, 0) < HEAD_DIM
    sw = _swap_row_halves(vt)
    one = jnp.ones_like(vt)
    pick = lambda a, b: jnp.where(top, a, b).astype(BF16)
    return ((pick(vt, one), pick(one, sw)), (pick(sw, one), pick(one, vt)))


def _attend(q_ref, jobs, sink_ref, l, cat_ref, batch):
    lane = lax.broadcasted_iota(jnp.int32, (Q_BLK, LANES), 1)
    lo = lane < HEAD_DIM
    top = lax.broadcasted_iota(jnp.int32, (LANES, Q_BLK), 0) < HEAD_DIM
    units = [(j, g, p, hf) for g in range(N_KV_HEADS) for p in range(2)
             for j in range(len(jobs)) for hf in range(2)]
    outs, terms = {}, {}
    for b0 in range(0, len(units), batch):
        batch_units = units[b0:b0 + batch]
        sinks = [sink_ref[l, g * 4 + p * 2 + hf] * LOG2E for _, g, p, hf in batch_units]
        scores = []
        for j, g, p, hf in batch_units:
            r0, segs = jobs[j]
            col = (2 * g + p) * LANES
            qpair = q_ref[r0:r0 + Q_BLK, col:col + LANES]
            sel = lo if hf == 0 else jnp.logical_not(lo)
            qm = jnp.where(sel, qpair, jnp.zeros_like(qpair))
            s_h = []
            for keys, _, bias in segs(g):
                s = _dot_t(keys, qm)
                s_h.append(s if bias is None else s + bias)
            scores.append(s_h)
        maxes = []
        for s_h, sink in zip(scores, sinks):
            m = jnp.full((1, Q_BLK), sink, F32)
            for s in s_h:
                m = jnp.maximum(m, jnp.max(s, axis=0, keepdims=True))
            maxes.append(m)
        for (j, g, p, hf), s_h, m, sink in zip(batch_units, scores, maxes, sinks):
            o = None
            for s, (_, vals, _) in zip(s_h, jobs[j][1](g)):
                part = _dot(vals[hf], jnp.exp2(s - m).astype(BF16))
                o = part if o is None else o + part
            outs[j, g, p, hf] = o
            terms[j, g, p, hf] = jnp.exp2(sink - m)
        for j, g, p, hf in batch_units:
            if hf == 0 or (j, g, p, 0) not in outs:
                continue
            o0, o1 = outs.pop((j, g, p, 0)), outs.pop((j, g, p, 1))
            num = jnp.where(top, o0, o1)
            den = jnp.where(top, o1 + terms[j, g, p, 1], o0 + terms[j, g, p, 0])
            r0, col = jobs[j][0], (2 * g + p) * LANES
            cat_ref[r0:r0 + Q_BLK, col:col + LANES] = (num / _swap_row_halves(den)).T.astype(BF16)


def _core_kernel(*refs, latent, l):
    if latent:
        (x_ref, mod_ref, nw_ref, q_ref, k_ref, v_ref, a_ref, p_ref, ck_ref, cv_ref,
         wo_ref, wp_ref, ps_ref, sink_ref, o_ref, cat_ref, pool_ref) = refs
    else:
        (x_ref, mod_ref, nw_ref, q_ref, k_ref, v_ref, a_ref, p_ref,
         wo_ref, wp_ref, ps_ref, sink_ref, o_ref, cat_ref) = refs
    n_rows = x_ref.shape[0]
    pool_cols = slice(ATTN_WIDTH + GM_WIDTH, D_MODEL)

    if latent:
        jq = pl.program_id(1)
        s_len = k_ref.shape[0]

        @pl.when(jq == 0)
        def _():
            pool_ref[...] = _pool(p_ref[...], wp_ref[...], ps_ref[...]).astype(BF16)

        ckk = _dup_halves(ck_ref[...])
        cvv = _head_values(cv_ref[...])
        jobs = []
        for s in range(n_rows // Q_BLK):
            q0 = jq * n_rows + s * Q_BLK
            start = pl.multiple_of(jnp.clip(q0 - WINDOW, 0, s_len - BAND), WINDOW)
            kk = _dup_halves(k_ref[pl.ds(start, BAND), :])
            vv = _head_values(v_ref[pl.ds(start, BAND), :])
            kpos = start + lax.broadcasted_iota(jnp.int32, (BAND, Q_BLK), 0)
            qpos = q0 + lax.broadcasted_iota(jnp.int32, (BAND, Q_BLK), 1)
            bias = jnp.where(jnp.abs(qpos - kpos) <= WINDOW, 0.0, NEG_INF)
            jobs.append((s * Q_BLK, lambda g, kk=kk, vv=vv, bias=bias: [
                (kk[g], vv[g], bias), (ckk[g], cvv[g], None)]))
        _attend(q_ref, jobs, sink_ref, l, cat_ref, LAT_HEAD_BATCH)
        cat_ref[:, pool_cols] = pool_ref[pl.ds(pl.multiple_of(jq * n_rows, n_rows), n_rows), :]
    else:
        jobs = []
        for s in range(n_rows // Q_BLK):
            rows = slice(s * Q_BLK, (s + 1) * Q_BLK)
            kk = _dup_halves(k_ref[s])
            vv = _head_values(v_ref[s])
            cat_ref[rows, pool_cols] = _pool(p_ref[rows, :], wp_ref[...], ps_ref[...]).astype(BF16)
            jobs.append((s * Q_BLK, lambda g, kk=kk, vv=vv: [(kk[g], vv[g], None)]))
        _attend(q_ref, jobs, sink_ref, l, cat_ref, CTX_HEAD_BATCH)

    cat_ref[:, ATTN_WIDTH:ATTN_WIDTH + GM_WIDTH] = a_ref[...]
    out = _dot(cat_ref[...], wo_ref[...])
    gate = mod_ref[0, 5:6, :]
    o_ref[...] = x_ref[...] + gate * _rms(out, nw_ref[3:4, :])


def _mixer_core(x, mod_all, norm_w, q, k, v, a, p, w_out, wp_bd, pool_scale, sink, l,
                cache=None):
    latent = cache is not None
    if latent:
        n_rows = CORE_LAT_ROWS
        grid = (N_LAT_SEQ, LAT_SEQ // n_rows)
        blk = lambda b, j: (T_CTX // n_rows + b * grid[1] + j, 0)
        cond = lambda b, j: (l, 1 + b, 0, 0)
        kv_spec = pl.BlockSpec((LAT_SEQ, KV_WIDTH), lambda b, j: (b, 0))
        p_spec = pl.BlockSpec((LAT_SEQ, POOL_WIDTH), lambda b, j: (T_CTX // LAT_SEQ + b, 0))
    else:
        n_rows = CORE_CTX_ROWS
        grid = (T_CTX // n_rows, 1)
        blk = lambda b, j: (b, 0)
        cond = lambda b, j: (l, 0, 0, 0)
        kv_spec = pl.BlockSpec((n_rows // CTX_SEQ, None, CTX_SEQ, KV_WIDTH),
                               lambda b, j: (b, l, 0, 0))
        p_spec = pl.BlockSpec((n_rows, POOL_WIDTH), blk)
    in_specs = [
        pl.BlockSpec((n_rows, D_MODEL), blk),
        pl.BlockSpec((None, 1, N_MOD, D_MODEL), cond),
        pl.BlockSpec((None, 6, D_MODEL), lambda b, j: (l, 0, 0)),
        pl.BlockSpec((n_rows, ATTN_WIDTH), blk),
        kv_spec,
        kv_spec,
        pl.BlockSpec((n_rows, GM_WIDTH), blk),
        p_spec,
    ]
    args = [x, mod_all, norm_w, q, k, v, a, p]
    if latent:
        ck, cv = cache
        in_specs += [
            pl.BlockSpec((None, None, PAST_LEN, KV_WIDTH), lambda b, j: (b, l, 0, 0)),
            pl.BlockSpec((None, None, PAST_LEN, KV_WIDTH), lambda b, j: (b, l, 0, 0)),
        ]
        args += [ck, cv]
    in_specs += [
        pl.BlockSpec((D_MODEL, D_MODEL), lambda b, j: (0, 0)),
        pl.BlockSpec((None, POOL_WIDTH, POOL_WIDTH), lambda b, j: (l, 0, 0)),
        pl.BlockSpec((None, 1, POOL_WIDTH), lambda b, j: (l, 0, 0)),
        pl.BlockSpec(memory_space=pltpu.SMEM),
    ]
    args += [w_out, wp_bd, pool_scale, sink]
    kern = functools.partial(_core_kernel, latent=latent, l=l)
    scratch = [pltpu.VMEM((n_rows, D_MODEL), BF16)]
    if latent:
        scratch.append(pltpu.VMEM((LAT_SEQ, POOL_WIDTH), BF16))
    return pl.pallas_call(
        kern,
        out_shape=jax.ShapeDtypeStruct((T_ALL, D_MODEL), F32),
        grid=grid,
        in_specs=in_specs,
        out_specs=pl.BlockSpec((n_rows, D_MODEL), blk),
        scratch_shapes=scratch,
        input_output_aliases={0: 0},
        compiler_params=pltpu.CompilerParams(
            dimension_semantics=("arbitrary", "arbitrary"), vmem_limit_bytes=VMEM_LIMIT),
        name="mixer_core_lat" if latent else "mixer_core_ctx",
    )(*args)


def _rope_tables():
    t = jnp.arange(LAT_SEQ, dtype=jnp.int32)
    row = (t // GRID_W).astype(F32)
    col = (t % GRID_W).astype(F32)
    nf = HEAD_DIM // 4
    inv = ROPE_BASE ** (-jnp.arange(nf, dtype=F32) / nf)
    ar = row[:, None] * inv
    ac = col[:, None] * inv
    cos64 = jnp.concatenate([jnp.cos(ar), jnp.cos(ar), jnp.cos(ac), jnp.cos(ac)], axis=-1)
    sin64 = jnp.concatenate([-jnp.sin(ar), jnp.sin(ar), -jnp.sin(ac), jnp.sin(ac)], axis=-1)
    return jnp.tile(cos64, (1, 2)), jnp.tile(sin64, (1, 2))


def kernel(x_prompt, x_sample, cache_k, cache_v, c, c_ctx, w_mod, b_mod, norm_w, w_in, w_out,
           attn_sink, w_spatial, b_spatial, w_pool, pool_scale, ffn_w1, ffn_w2):
    cond = jnp.concatenate(
        [c_ctx[None, :], c, jnp.zeros((N_COND - 1 - N_LAT_SEQ, D_MODEL), F32)], axis=0)
    mod_all = _adaln(cond, w_mod, b_mod).reshape(DEPTH, N_COND, N_MOD, D_MODEL)

    w1 = ffn_w1[0, 0].astype(BF16)
    w2 = ffn_w2[0, 0].astype(BF16)
    w_in_b = w_in[0].astype(BF16)
    ws_cat = w_spatial.transpose(0, 2, 1, 3).reshape(DEPTH, CHUNK, GM_HEADS * CHUNK).astype(BF16)
    bz = jnp.repeat(b_spatial.transpose(0, 2, 1), GM_DIM, axis=2)
    eye = jnp.eye(len(POOL_HALF_WINDOWS), dtype=F32)
    wp_bd = jnp.einsum('lgij,gh->lgihj', w_pool, eye).reshape(DEPTH, POOL_WIDTH, POOL_WIDTH).astype(BF16)
    ps = pool_scale.reshape(DEPTH, 1, POOL_WIDTH)
    cos_t, sin_t = _rope_tables()
    ck = cache_k.reshape(N_LAT_SEQ, DEPTH, PAST_LEN, KV_WIDTH)
    cv = cache_v.reshape(N_LAT_SEQ, DEPTH, PAST_LEN, KV_WIDTH)

    xs = (x_prompt.reshape(T_CTX, D_MODEL), x_sample.reshape(T_LAT, D_MODEL))
    caches = None
    for l in range(DEPTH):
        side = (_side_job(ffn_w1, (l, 1), W1_SIDE_BLOCKS), _side_job(ffn_w2, (l, 1), W2_SIDE_BLOCKS),
                _side_job(w_out, (l,), WIO_SIDE_BLOCKS))
        (x,), (w1, w2, w_out_b) = _ffn_half(xs, mod_all, norm_w, w1, w2, l, 0, side=side)
        q, kc, vc, kl, vl, a, p = _mixer_proj(
            x, mod_all, norm_w, w_in_b, cos_t, sin_t, ws_cat, bz, l, caches=caches)
        caches = (kc, vc)
        x = _mixer_core(x, mod_all, norm_w, q, kc, vc, a, p, w_out_b, wp_bd, ps, attn_sink, l)
        x = _mixer_core(x, mod_all, norm_w, q, kl, vl, a, p, w_out_b, wp_bd, ps, attn_sink, l,
                        cache=(ck, cv))
        if l < DEPTH - 1:
            side = (_side_job(ffn_w1, (l + 1, 0), W1_SIDE_BLOCKS),
                    _side_job(ffn_w2, (l + 1, 0), W2_SIDE_BLOCKS),
                    _side_job(w_in, (l + 1,), WIO_SIDE_BLOCKS))
            xs, (w1, w2, w_in_b) = _ffn_half((x,), mod_all, norm_w, w1, w2, l, 1, side=side)
        else:
            xs, _ = _ffn_half((x,), mod_all, norm_w, w1, w2, l, 1, split_out=True)
    y_prompt = xs[0].reshape(N_CTX_SEQ, CTX_SEQ, D_MODEL)
    y_sample = xs[1].reshape(N_LAT_SEQ, LAT_SEQ, D_MODEL)
    cache_shape = (N_CTX_SEQ, DEPTH, CTX_SEQ, N_KV_HEADS, HEAD_DIM)
    return y_prompt, y_sample, caches[0].reshape(cache_shape), caches[1].reshape(cache_shape)
```

```python
import functools

import jax
import jax.numpy as jnp
from jax import lax
from jax.experimental import pallas as pl
from jax.experimental.pallas import tpu as pltpu

D_MODEL = 1024
N_CTX_SEQ = 32
CTX_SEQ = 256
DEPTH = 4
N_LAT_SEQ = 2
LAT_SEQ = 2048
PAST_LEN = 512
GRID_W = 64
HEAD_DIM = 64
ATTN_WIDTH = 512
N_HEADS = 8
N_KV_HEADS = 2
KV_WIDTH = 128
WINDOW = 128
GM_WIDTH = 256
GM_HEADS = 4
GM_DIM = 64
CHUNK = 128
POOL_WIDTH = 256
POOL_HALF_WINDOWS = (1, 2, 4, 8)
POOL_DIM = 64
IN_WIDTH = 1536
D_FF = 2816
N_MOD = 9
EPS = 1e-6
ROPE_BASE = 10000.0
NEG_INF = -1e30
ATTN_SCALE = HEAD_DIM ** -0.5

T_CTX = N_CTX_SEQ * CTX_SEQ
T_LAT = N_LAT_SEQ * LAT_SEQ
T_ALL = T_CTX + T_LAT
N_COND = 8

LANES = 128
VMEM_LIMIT = 56 * 1024 * 1024

FFN_TM = 512
FFN_CHUNKS = (512, 512, 512, 512, 512, 256)
W1_SIDE_BLOCKS = 16
W2_SIDE_BLOCKS = 22
WIO_SIDE_BLOCKS = 16
PROJ_TM = 1024
Q_BLK = 256
BAND = Q_BLK + 2 * WINDOW
CORE_CTX_ROWS = 1024
CORE_LAT_ROWS = 512
LOG2E = 1.4426950408889634
LAT_HEAD_BATCH = 16
CTX_HEAD_BATCH = 16
MOD_TN = 2304

BF16 = jnp.bfloat16
F32 = jnp.float32


def _dot(a, b):
    return jnp.dot(a, b, preferred_element_type=F32)


def _dot_t(a, b):
    return lax.dot_general(a, b, (((1,), (1,)), ((), ())), preferred_element_type=F32)


def _rms(x, g):
    return x * lax.rsqrt(jnp.mean(x * x, axis=-1, keepdims=True) + EPS) * g


def _zero_from(v):
    bits = lax.bitcast_convert_type(v, jnp.uint32)
    return ((bits >> 16) >> 16).astype(F32)


def _cond_row(tile, tm):
    n_ctx = T_CTX // tm
    per_seq = LAT_SEQ // tm
    return jnp.where(tile < n_ctx, 0, 1 + (tile - n_ctx) // per_seq)


def _mod_kernel(c_ref, w_ref, b_ref, o_ref):
    c = c_ref[...]
    s = jax.nn.silu(c).astype(BF16)
    o_ref[...] = _dot(s, w_ref[...].astype(BF16)) + b_ref[...]


def _adaln(cond, w_mod, b_mod):
    n_out = N_MOD * D_MODEL
    return pl.pallas_call(
        _mod_kernel,
        out_shape=jax.ShapeDtypeStruct((DEPTH, N_COND, n_out), F32),
        grid=(DEPTH, n_out // MOD_TN),
        in_specs=[
            pl.BlockSpec((N_COND, D_MODEL), lambda l, n: (0, 0)),
            pl.BlockSpec((None, D_MODEL, MOD_TN), lambda l, n: (l, 0, n)),
            pl.BlockSpec((None, 1, MOD_TN), lambda l, n: (l, 0, n)),
        ],
        out_specs=pl.BlockSpec((None, N_COND, MOD_TN), lambda l, n: (l, 0, n)),
        compiler_params=pltpu.CompilerParams(
            dimension_semantics=("arbitrary", "arbitrary"), vmem_limit_bytes=VMEM_LIMIT),
        name="adaln",
    )(cond, w_mod, b_mod.reshape(DEPTH, 1, n_out))


def _ffn_kernel(*refs, j, pre, post, split_in, split_out, n_side):
    n_x = 2 if split_in else 1
    n_o = 2 if split_out else 1
    xp_refs = refs[:n_x]
    xn_refs = refs[n_x:2 * n_x]
    modp_ref, modn_ref, nw_ref, w1_ref, w2_ref = refs[2 * n_x:2 * n_x + 5]
    pos = 2 * n_x + 5
    side_in = refs[pos:pos + n_side]
    o_refs = refs[pos + n_side:pos + n_side + n_o]
    side_out = refs[pos + n_side + n_o:pos + 2 * n_side + n_o]
    scratch = refs[pos + 2 * n_side + n_o:]
    h_refs, acc_ref = scratch[:2], scratch[2]
    i = pl.program_id(0)
    n_tiles = T_ALL // FFN_TM
    n_ctx = T_CTX // FFN_TM

    def load_x(x_refs, tile, rows):
        if split_in:
            return jnp.where(tile < n_ctx, x_refs[0][rows, :], x_refs[1][rows, :])
        return x_refs[0][rows, :]

    def prologue(x_refs, mod_ref, tile, rows):
        shift = mod_ref[0, j:j + 1, :]
        scale = mod_ref[0, j + 1:j + 2, :]
        x = load_x(x_refs, tile, rows)
        return _rms(x, nw_ref[pre:pre + 1, :] * (1 + scale)) + shift

    def epilogue(rows):
        gate = modp_ref[0, j + 2:j + 3, :]
        y = (load_x(xp_refs, i - 1, rows)
             + _rms(acc_ref[rows, :], nw_ref[post:post + 1, :] * (0.5 * gate)))
        if split_out:
            scratch[3][rows, :] = y
        else:
            o_refs[0][rows, :] = y
        return y

    def store_split():
        prev_ctx = i - 1 < n_ctx

        @pl.when(prev_ctx)
        def _():
            o_refs[0][...] = scratch[3][...]

        @pl.when(jnp.logical_not(prev_ctx))
        def _():
            o_refs[1][...] = scratch[3][...]

    all_rows = slice(0, FFN_TM)
    halves = (slice(0, FFN_TM // 2), slice(FFN_TM // 2, FFN_TM))

    @pl.when(i == 0)
    def _():
        h_refs[0][...] = prologue(xp_refs, modp_ref, 0, all_rows).astype(BF16)
        acc_ref[...] = jnp.zeros_like(acc_ref)

    def main(h_cur, h_nxt):
        def vpu_piece(c):
            if c in (0, 1):
                return _zero_from(epilogue(halves[c])[:, 0:1])
            if c in (2, 3):
                rows = halves[c - 2]
                hf = prologue(xn_refs, modn_ref, i + 1, rows)
                h_nxt[rows, :] = hf.astype(BF16)
                return _zero_from(hf[:, 0:1])
            if c == 4:
                for s_in, s_out in zip(side_in, side_out):
                    s_out[...] = s_in[...].astype(BF16)
            return None

        acc = None
        off = 0
        tie = None
        for c, tk in enumerate(FFN_CHUNKS):
            g = _dot(h_cur[...], w1_ref[:, off:off + tk])
            u = _dot(h_cur[...], w1_ref[:, D_FF + off:D_FF + off + tk])
            if tie is not None:
                tied = halves[(c - 1) % 2]
                u = jnp.concatenate(
                    [u[r, :] + tie if r is tied else u[r, :] for r in halves], axis=0)
            a = (jax.nn.silu(g) * u).astype(BF16)
            part = _dot(a, w2_ref[off:off + tk, :])
            acc = part if acc is None else acc + part
            off += tk
            tie = vpu_piece(c)
        acc_ref[...] = acc

    for par in range(2):
        @pl.when(jnp.logical_and(i < n_tiles, lax.rem(i, 2) == par))
        def _(par=par):
            main(h_refs[par], h_refs[1 - par])

    @pl.when(i == n_tiles)
    def _():
        epilogue(all_rows)

    if split_out:
        store_split()


def _side_job(arr, lead, n_blk):
    rows, cols = arr.shape[-2:]
    blk = rows // n_blk
    nones = (None,) * len(lead)
    in_spec = pl.BlockSpec(nones + (blk, cols), lambda i: lead + (jnp.minimum(i, n_blk - 1), 0))
    out_spec = pl.BlockSpec((blk, cols), lambda i: (jnp.minimum(i, n_blk - 1), 0))
    return arr, in_spec, out_spec, jax.ShapeDtypeStruct((rows, cols), BF16)


def _ffn_half(xs, mod_all, norm_w, w1, w2, l, half, split_out=False, side=()):
    j, pre, post = (0, 0, 1) if half == 0 else (6, 4, 5)
    tm = FFN_TM
    n_tiles = T_ALL // tm
    n_ctx = T_CTX // tm
    n_lat = T_LAT // tm
    split_in = len(xs) == 2
    tile = (tm, D_MODEL)

    def specs(shift):
        t = lambda i: jnp.clip(i + shift, 0, n_tiles - 1)
        if split_in:
            return [pl.BlockSpec(tile, lambda i: (jnp.minimum(t(i), n_ctx - 1), 0)),
                    pl.BlockSpec(tile, lambda i: (jnp.clip(t(i) - n_ctx, 0, n_lat - 1), 0))]
        return [pl.BlockSpec(tile, lambda i: (t(i), 0))]

    def mod_spec(shift):
        t = lambda i: jnp.clip(i + shift, 0, n_tiles - 1)
        return pl.BlockSpec((None, 1, N_MOD, D_MODEL), lambda i: (l, _cond_row(t(i), tm), 0, 0))

    prev = lambda i: jnp.clip(i - 1, 0, n_tiles - 1)
    if split_out:
        out_shape = [jax.ShapeDtypeStruct((T_CTX, D_MODEL), F32),
                     jax.ShapeDtypeStruct((T_LAT, D_MODEL), F32)]
        out_specs = [pl.BlockSpec(tile, lambda i: (jnp.minimum(prev(i), n_ctx - 1), 0)),
                     pl.BlockSpec(tile, lambda i: (jnp.clip(prev(i) - n_ctx, 0, n_lat - 1), 0))]
    else:
        out_shape = [jax.ShapeDtypeStruct((T_ALL, D_MODEL), F32)]
        out_specs = [pl.BlockSpec(tile, lambda i: (prev(i), 0))]
    scratch = [pltpu.VMEM(tile, BF16), pltpu.VMEM(tile, BF16), pltpu.VMEM(tile, F32)]
    if split_out:
        scratch.append(pltpu.VMEM(tile, F32))
    whole = lambda i: (0, 0)
    outs = pl.pallas_call(
        functools.partial(_ffn_kernel, j=j, pre=pre, post=post, split_in=split_in,
                          split_out=split_out, n_side=len(side)),
        out_shape=out_shape + [s[3] for s in side],
        grid=(n_tiles + 1,),
        in_specs=specs(-1) + specs(1) + [
            mod_spec(-1),
            mod_spec(1),
            pl.BlockSpec((None, 6, D_MODEL), lambda i: (l, 0, 0)),
            pl.BlockSpec((D_MODEL, 2 * D_FF), whole, pipeline_mode=pl.Buffered(1)),
            pl.BlockSpec((D_FF, D_MODEL), whole, pipeline_mode=pl.Buffered(1)),
        ] + [s[1] for s in side],
        out_specs=out_specs + [s[2] for s in side],
        scratch_shapes=scratch,
        compiler_params=pltpu.CompilerParams(
            dimension_semantics=("arbitrary",), vmem_limit_bytes=VMEM_LIMIT),
        name="ffn_half",
    )(*xs, *xs, mod_all, mod_all, norm_w, w1, w2, *[s[0] for s in side])
    n_o = len(out_shape)
    return outs[:n_o], outs[n_o:]


def _swap16(x):
    w = x.shape[-1]
    lane = lax.broadcasted_iota(jnp.int32, x.shape, x.ndim - 1)
    first = (lane & 31) < 16
    return jnp.where(first, pltpu.roll(x, w - 16, axis=x.ndim - 1), pltpu.roll(x, 16, axis=x.ndim - 1))


def _rope(x, cos_t, sin_t):
    cols = [x[:, j:j + LANES] for j in range(0, x.shape[-1], LANES)]
    cols = [c * cos_t + _swap16(c) * sin_t for c in cols]
    return cols[0] if len(cols) == 1 else jnp.concatenate(cols, axis=-1)


def _proj_kernel(x_ref, mod_ref, nw_ref, w_ref, cos_ref, sin_ref, ws_ref, bz_ref, *rest):
    q_ref, kc_ref, vc_ref, kl_ref, vl_ref, a_ref, p_ref = rest[-7:]
    i = pl.program_id(0)
    tm = x_ref.shape[0]
    half = tm // 2
    shift = mod_ref[0, 3:4, :]
    scale = mod_ref[0, 4:5, :]
    r = lax.broadcasted_iota(jnp.int32, (GM_WIDTH, GM_WIDTH), 0)
    c = lax.broadcasted_iota(jnp.int32, (GM_WIDTH, GM_WIDTH), 1)
    grp_mean = jnp.where((r >> 6) == (c >> 6), 1.0 / GM_DIM, 0.0).astype(BF16)
    lane = lax.broadcasted_iota(jnp.int32, (CHUNK, GM_WIDTH), 1)
    ws = ws_ref[...]
    bz = bz_ref[...]
    mix_cols = ATTN_WIDTH + 2 * KV_WIDTH

    def pre(rows, after=None):
        x = x_ref[rows, :]
        if after is not None:
            x = x + after
        return _rms(x, nw_ref[2:3, :] * (1 + scale)) + shift

    def gmlp_pool(rows, gu, gv, pool_in):
        p_ref[rows, :] = pool_in
        sq = gv * gv
        sq_hi = sq.astype(BF16)
        sq_lo = (sq - sq_hi.astype(F32)).astype(BF16)
        ms = _dot(sq_hi, grp_mean) + _dot(sq_lo, grp_mean)
        vh = (gv * lax.rsqrt(ms + EPS)).astype(BF16)
        outs = []
        for n in range(half // CHUNK):
            vn = vh[n * CHUNK:(n + 1) * CHUNK, :]
            bd = jnp.concatenate(
                [jnp.where((lane >> 6) == hh, vn, jnp.zeros_like(vn)) for hh in range(GM_HEADS)],
                axis=0)
            outs.append(gu[n * CHUNK:(n + 1) * CHUNK, :] * (_dot(ws, bd) + bz))
        a = jnp.concatenate(outs, axis=0)
        a_ref[rows, :] = a.astype(BF16)
        return a

    rows_a, rows_b = slice(0, half), slice(half, tm)
    h_a = pre(rows_a)
    h_b = pre(rows_b, _zero_from(h_a[:, 0:1]))
    proj_a = _dot(h_a.astype(BF16), w_ref[...])
    a_a = gmlp_pool(rows_a, proj_a[:, 768:1024], proj_a[:, 1024:1280], proj_a[:, 1280:1536])
    qkv_b = _dot(h_b.astype(BF16), w_ref[:, 0:mix_cols])
    h_b2 = (h_b + _zero_from(a_a[:, 0:1])).astype(BF16)
    mix_b = _dot(h_b2, w_ref[:, mix_cols:])
    gmlp_pool(rows_b, mix_b[:, 0:256], mix_b[:, 256:512], mix_b[:, 512:768])

    q = [proj_a[:, 0:ATTN_WIDTH] * (ATTN_SCALE * LOG2E), qkv_b[:, 0:ATTN_WIDTH] * (ATTN_SCALE * LOG2E)]
    k = [proj_a[:, ATTN_WIDTH:ATTN_WIDTH + KV_WIDTH], qkv_b[:, ATTN_WIDTH:ATTN_WIDTH + KV_WIDTH]]
    v = [proj_a[:, ATTN_WIDTH + KV_WIDTH:mix_cols], qkv_b[:, ATTN_WIDTH + KV_WIDTH:mix_cols]]
    is_lat = i >= T_CTX // tm

    @pl.when(is_lat)
    def _():
        for n, rows in enumerate((rows_a, rows_b)):
            cos_t = cos_ref[rows, :]
            sin_t = sin_ref[rows, :]
            q_ref[rows, :] = _rope(q[n], cos_t, sin_t).astype(BF16)
            kl_ref[rows, :] = _rope(k[n], cos_t, sin_t)
            vl_ref[rows, :] = v[n]

    @pl.when(jnp.logical_not(is_lat))
    def _():
        for n, rows in enumerate((rows_a, rows_b)):
            q_ref[rows, :] = q[n].astype(BF16)
            for s in range(half // CTX_SEQ):
                seq = n * (half // CTX_SEQ) + s
                kc_ref[seq] = k[n][s * CTX_SEQ:(s + 1) * CTX_SEQ, :]
                vc_ref[seq] = v[n][s * CTX_SEQ:(s + 1) * CTX_SEQ, :]


def _mixer_proj(x, mod_all, norm_w, w_in, cos_t, sin_t, ws_cat, bz, l, caches=None):
    tm = PROJ_TM
    n_ctx = T_CTX // tm
    per_seq = LAT_SEQ // tm
    seq_per_tile = tm // CTX_SEQ

    def tbl_map(i):
        return (lax.rem(jnp.maximum(i - n_ctx, 0), per_seq), 0)

    row = lambda i: (i, 0)
    lat_row = lambda i: (jnp.maximum(i - n_ctx, 0), 0)
    cache_blk = pl.BlockSpec((seq_per_tile, None, CTX_SEQ, KV_WIDTH),
                             lambda i: (jnp.minimum(i, n_ctx - 1), l, 0, 0))
    cache_shape = jax.ShapeDtypeStruct((N_CTX_SEQ, DEPTH, CTX_SEQ, KV_WIDTH), F32)
    extra_specs, extra_args, aliases = [], [], {}
    if caches is not None:
        extra_specs = [pl.BlockSpec(memory_space=pl.ANY)] * 2
        extra_args = list(caches)
        aliases = {8: 1, 9: 2}
    return pl.pallas_call(
        _proj_kernel,
        out_shape=(
            jax.ShapeDtypeStruct((T_ALL, ATTN_WIDTH), BF16),
            cache_shape,
            cache_shape,
            jax.ShapeDtypeStruct((T_LAT, KV_WIDTH), F32),
            jax.ShapeDtypeStruct((T_LAT, KV_WIDTH), F32),
            jax.ShapeDtypeStruct((T_ALL, GM_WIDTH), BF16),
            jax.ShapeDtypeStruct((T_ALL, POOL_WIDTH), F32),
        ),
        grid=(T_ALL // tm,),
        input_output_aliases=aliases,
        in_specs=[
            pl.BlockSpec((tm, D_MODEL), row),
            pl.BlockSpec((None, 1, N_MOD, D_MODEL), lambda i: (l, _cond_row(i, tm), 0, 0)),
            pl.BlockSpec((None, 6, D_MODEL), lambda i: (l, 0, 0)),
            pl.BlockSpec((D_MODEL, IN_WIDTH), lambda i: (0, 0)),
            pl.BlockSpec((tm, LANES), tbl_map),
            pl.BlockSpec((tm, LANES), tbl_map),
            pl.BlockSpec((None, CHUNK, GM_HEADS * CHUNK), lambda i: (l, 0, 0)),
            pl.BlockSpec((None, CHUNK, GM_WIDTH), lambda i: (l, 0, 0)),
        ] + extra_specs,
        out_specs=(
            pl.BlockSpec((tm, ATTN_WIDTH), row),
            cache_blk,
            cache_blk,
            pl.BlockSpec((tm, KV_WIDTH), lat_row),
            pl.BlockSpec((tm, KV_WIDTH), lat_row),
            pl.BlockSpec((tm, GM_WIDTH), row),
            pl.BlockSpec((tm, POOL_WIDTH), row),
        ),
        compiler_params=pltpu.CompilerParams(
            dimension_semantics=("arbitrary",), vmem_limit_bytes=VMEM_LIMIT),
        name="mixer_proj",
    )(x, mod_all, norm_w, w_in, cos_t, sin_t, ws_cat, bz, *extra_args)


def _pool_inv_count(s_len):
    shape = (s_len, POOL_WIDTH)
    row = lax.broadcasted_iota(jnp.int32, shape, 0)
    grp = lax.broadcasted_iota(jnp.int32, shape, 1) >> 6
    half = jnp.where(grp == 0, POOL_HALF_WINDOWS[0],
                     jnp.where(grp == 1, POOL_HALF_WINDOWS[1],
                               jnp.where(grp == 2, POOL_HALF_WINDOWS[2], POOL_HALF_WINDOWS[3])))
    cnt = (jnp.minimum(row + half, s_len) - jnp.maximum(row - half, 0)).astype(F32)
    return 1.0 / cnt


def _pool(x, wp, scale, inv_cnt):
    s_len = x.shape[0]

    def down(y, s):
        row = lax.broadcasted_iota(jnp.int32, y.shape, 0)
        return jnp.where(row >= s, pltpu.roll(y, s, axis=0), 0.0)

    def up(y, s):
        row = lax.broadcasted_iota(jnp.int32, y.shape, 0)
        return jnp.where(row < s_len - s, pltpu.roll(y, s_len - s, axis=0), 0.0)

    first = lax.broadcasted_iota(jnp.int32, (s_len, LANES), 1) < POOL_DIM
    back1 = down(x, 1)
    back2 = back1 + down(back1, 1)
    fwd2 = x + up(x, 1)
    lo = jnp.where(first, (back1 + x)[:, :LANES], (back2 + fwd2)[:, :LANES])
    back4 = back2[:, LANES:] + down(back2[:, LANES:], 2)
    fwd4 = fwd2[:, LANES:] + up(fwd2[:, LANES:], 2)
    back8 = back4 + down(back4, 4)
    fwd8 = fwd4 + up(fwd4, 4)
    hi = jnp.where(first, back4 + fwd4, back8 + fwd8)
    tot = jnp.concatenate([lo, hi], axis=1)
    d = (tot * inv_cnt - x).astype(BF16)
    return _dot(d, wp) * scale


def _dup_halves(t):
    lane = lax.broadcasted_iota(jnp.int32, t.shape, 1)
    lo = lane < HEAD_DIM
    sw = pltpu.roll(t, HEAD_DIM, axis=1)
    return (jnp.where(lo, t, sw).astype(BF16), jnp.where(lo, sw, t).astype(BF16))


def _swap_row_halves(t):
    half = t.shape[0] // 2
    return jnp.concatenate([t[half:], t[:half]], axis=0)


def _head_values(v):
    vt = v.T
    top = lax.broadcasted_iota(jnp.int32, vt.shape, 0) < HEAD_DIM
    sw = _swap_row_halves(vt)
    one = jnp.ones_like(vt)
    pick = lambda a, b: jnp.where(top, a, b).astype(BF16)
    return ((pick(vt, one), pick(one, sw)), (pick(sw, one), pick(one, vt)))


def _attend(q_ref, jobs, sink_ref, l, cat_ref, batch):
    lane = lax.broadcasted_iota(jnp.int32, (Q_BLK, LANES), 1)
    lo = lane < HEAD_DIM
    top = lax.broadcasted_iota(jnp.int32, (LANES, Q_BLK), 0) < HEAD_DIM
    units = [(j, g, p, hf) for g in range(N_KV_HEADS) for p in range(2)
             for j in range(len(jobs)) for hf in range(2)]
    outs, terms = {}, {}
    for b0 in range(0, len(units), batch):
        batch_units = units[b0:b0 + batch]
        sinks = [sink_ref[l, g * 4 + p * 2 + hf] * LOG2E for _, g, p, hf in batch_units]
        scores = []
        for j, g, p, hf in batch_units:
            r0, segs = jobs[j]
            col = (2 * g + p) * LANES
            qpair = q_ref[r0:r0 + Q_BLK, col:col + LANES]
            sel = lo if hf == 0 else jnp.logical_not(lo)
            qm = jnp.where(sel, qpair, jnp.zeros_like(qpair))
            s_h = []
            for keys, _, bias in segs(g):
                s = _dot_t(keys, qm)
                s_h.append(s if bias is None else s + bias)
            scores.append(s_h)
        maxes = []
        for s_h, sink in zip(scores, sinks):
            m = jnp.full((1, Q_BLK), sink, F32)
            for s in s_h:
                m = jnp.maximum(m, jnp.max(s, axis=0, keepdims=True))
            maxes.append(m)
        for (j, g, p, hf), s_h, m, sink in zip(batch_units, scores, maxes, sinks):
            o = None
            for s, (_, vals, _) in zip(s_h, jobs[j][1](g)):
                part = _dot(vals[hf], jnp.exp2(s - m).astype(BF16))
                o = part if o is None else o + part
            outs[j, g, p, hf] = o
            terms[j, g, p, hf] = jnp.exp2(sink - m)
        for j, g, p, hf in batch_units:
            if hf == 0 or (j, g, p, 0) not in outs:
                continue
            o0, o1 = outs.pop((j, g, p, 0)), outs.pop((j, g, p, 1))
            num = jnp.where(top, o0, o1)
            den = jnp.where(top, o1 + terms[j, g, p, 1], o0 + terms[j, g, p, 0])
            r0, col = jobs[j][0], (2 * g + p) * LANES
            cat_ref[r0:r0 + Q_BLK, col:col + LANES] = (num / _swap_row_halves(den)).T.astype(BF16)


def _core_kernel(*refs, latent, l):
    if latent:
        (x_ref, mod_ref, nw_ref, q_ref, k_ref, v_ref, a_ref, p_ref, ck_ref, cv_ref,
         wo_ref, wp_ref, ps_ref, sink_ref, o_ref, cat_ref, pool_ref) = refs
    else:
        (x_ref, mod_ref, nw_ref, q_ref, k_ref, v_ref, a_ref, p_ref,
         wo_ref, wp_ref, ps_ref, sink_ref, o_ref, cat_ref) = refs
    n_rows = x_ref.shape[0]
    pool_cols = slice(ATTN_WIDTH + GM_WIDTH, D_MODEL)

    if latent:
        jq = pl.program_id(1)
        s_len = k_ref.shape[0]

        @pl.when(jq == 0)
        def _():
            pool_ref[...] = _pool(p_ref[...], wp_ref[...], ps_ref[...],
                                  _pool_inv_count(s_len)).astype(BF16)

        ckk = _dup_halves(ck_ref[...])
        cvv = _head_values(cv_ref[...])
        jobs = []
        for s in range(n_rows // Q_BLK):
            q0 = jq * n_rows + s * Q_BLK
            start = pl.multiple_of(jnp.clip(q0 - WINDOW, 0, s_len - BAND), WINDOW)
            kk = _dup_halves(k_ref[pl.ds(start, BAND), :])
            vv = _head_values(v_ref[pl.ds(start, BAND), :])
            kpos = start + lax.broadcasted_iota(jnp.int32, (BAND, Q_BLK), 0)
            qpos = q0 + lax.broadcasted_iota(jnp.int32, (BAND, Q_BLK), 1)
            bias = jnp.where(jnp.abs(qpos - kpos) <= WINDOW, 0.0, NEG_INF)
            jobs.append((s * Q_BLK, lambda g, kk=kk, vv=vv, bias=bias: [
                (kk[g], vv[g], bias), (ckk[g], cvv[g], None)]))
        _attend(q_ref, jobs, sink_ref, l, cat_ref, LAT_HEAD_BATCH)
        cat_ref[:, pool_cols] = pool_ref[pl.ds(pl.multiple_of(jq * n_rows, n_rows), n_rows), :]
    else:
        jobs = []
        inv_cnt = _pool_inv_count(Q_BLK)
        for s in range(n_rows // Q_BLK):
            rows = slice(s * Q_BLK, (s + 1) * Q_BLK)
            kk = _dup_halves(k_ref[s])
            vv = _head_values(v_ref[s])
            cat_ref[rows, pool_cols] = _pool(
                p_ref[rows, :], wp_ref[...], ps_ref[...], inv_cnt).astype(BF16)
            jobs.append((s * Q_BLK, lambda g, kk=kk, vv=vv: [(kk[g], vv[g], None)]))
        _attend(q_ref, jobs, sink_ref, l, cat_ref, CTX_HEAD_BATCH)

    cat_ref[:, ATTN_WIDTH:ATTN_WIDTH + GM_WIDTH] = a_ref[...]
    out = _dot(cat_ref[...], wo_ref[...])
    gate = mod_ref[0, 5:6, :]
    o_ref[...] = x_ref[...] + _rms(out, nw_ref[3:4, :] * gate)


def _mixer_core(x, mod_all, norm_w, q, k, v, a, p, w_out, wp_bd, pool_scale, sink, l,
                cache=None):
    latent = cache is not None
    if latent:
        n_rows = CORE_LAT_ROWS
        grid = (N_LAT_SEQ, LAT_SEQ // n_rows)
        blk = lambda b, j: (T_CTX // n_rows + b * grid[1] + j, 0)
        cond = lambda b, j: (l, 1 + b, 0, 0)
        kv_spec = pl.BlockSpec((LAT_SEQ, KV_WIDTH), lambda b, j: (b, 0))
        p_spec = pl.BlockSpec((LAT_SEQ, POOL_WIDTH), lambda b, j: (T_CTX // LAT_SEQ + b, 0))
    else:
        n_rows = CORE_CTX_ROWS
        grid = (T_CTX // n_rows, 1)
        blk = lambda b, j: (b, 0)
        cond = lambda b, j: (l, 0, 0, 0)
        kv_spec = pl.BlockSpec((n_rows // CTX_SEQ, None, CTX_SEQ, KV_WIDTH),
                               lambda b, j: (b, l, 0, 0))
        p_spec = pl.BlockSpec((n_rows, POOL_WIDTH), blk)
    in_specs = [
        pl.BlockSpec((n_rows, D_MODEL), blk),
        pl.BlockSpec((None, 1, N_MOD, D_MODEL), cond),
        pl.BlockSpec((None, 6, D_MODEL), lambda b, j: (l, 0, 0)),
        pl.BlockSpec((n_rows, ATTN_WIDTH), blk),
        kv_spec,
        kv_spec,
        pl.BlockSpec((n_rows, GM_WIDTH), blk),
        p_spec,
    ]
    args = [x, mod_all, norm_w, q, k, v, a, p]
    if latent:
        ck, cv = cache
        in_specs += [
            pl.BlockSpec((None, None, PAST_LEN, KV_WIDTH), lambda b, j: (b, l, 0, 0)),
            pl.BlockSpec((None, None, PAST_LEN, KV_WIDTH), lambda b, j: (b, l, 0, 0)),
        ]
        args += [ck, cv]
    in_specs += [
        pl.BlockSpec((D_MODEL, D_MODEL), lambda b, j: (0, 0)),
        pl.BlockSpec((None, POOL_WIDTH, POOL_WIDTH), lambda b, j: (l, 0, 0)),
        pl.BlockSpec((None, 1, POOL_WIDTH), lambda b, j: (l, 0, 0)),
        pl.BlockSpec(memory_space=pltpu.SMEM),
    ]
    args += [w_out, wp_bd, pool_scale, sink]
    kern = functools.partial(_core_kernel, latent=latent, l=l)
    scratch = [pltpu.VMEM((n_rows, D_MODEL), BF16)]
    if latent:
        scratch.append(pltpu.VMEM((LAT_SEQ, POOL_WIDTH), BF16))
    return pl.pallas_call(
        kern,
        out_shape=jax.ShapeDtypeStruct((T_ALL, D_MODEL), F32),
        grid=grid,
        in_specs=in_specs,
        out_specs=pl.BlockSpec((n_rows, D_MODEL), blk),
        scratch_shapes=scratch,
        input_output_aliases={0: 0},
        compiler_params=pltpu.CompilerParams(
            dimension_semantics=("arbitrary", "arbitrary"), vmem_limit_bytes=VMEM_LIMIT),
        name="mixer_core_lat" if latent else "mixer_core_ctx",
    )(*args)


def _rope_tables():
    t = jnp.arange(LAT_SEQ, dtype=jnp.int32)
    row = (t // GRID_W).astype(F32)
    col = (t % GRID_W).astype(F32)
    nf = HEAD_DIM // 4
    inv = ROPE_BASE ** (-jnp.arange(nf, dtype=F32) / nf)
    ar = row[:, None] * inv
    ac = col[:, None] * inv
    cos64 = jnp.concatenate([jnp.cos(ar), jnp.cos(ar), jnp.cos(ac), jnp.cos(ac)], axis=-1)
    sin64 = jnp.concatenate([-jnp.sin(ar), jnp.sin(ar), -jnp.sin(ac), jnp.sin(ac)], axis=-1)
    return jnp.tile(cos64, (1, 2)), jnp.tile(sin64, (1, 2))


def kernel(x_prompt, x_sample, cache_k, cache_v, c, c_ctx, w_mod, b_mod, norm_w, w_in, w_out,
           attn_sink, w_spatial, b_spatial, w_pool, pool_scale, ffn_w1, ffn_w2):
    cond = jnp.concatenate(
        [c_ctx[None, :], c, jnp.zeros((N_COND - 1 - N_LAT_SEQ, D_MODEL), F32)], axis=0)
    mod_all = _adaln(cond, w_mod, b_mod).reshape(DEPTH, N_COND, N_MOD, D_MODEL)

    w1 = ffn_w1[0, 0].astype(BF16)
    w2 = ffn_w2[0, 0].astype(BF16)
    w_in_b = w_in[0].astype(BF16)
    ws_cat = w_spatial.transpose(0, 2, 1, 3).reshape(DEPTH, CHUNK, GM_HEADS * CHUNK).astype(BF16)
    bz = jnp.repeat(b_spatial.transpose(0, 2, 1), GM_DIM, axis=2)
    eye = jnp.eye(len(POOL_HALF_WINDOWS), dtype=F32)
    wp_bd = jnp.einsum('lgij,gh->lgihj', w_pool, eye).reshape(DEPTH, POOL_WIDTH, POOL_WIDTH).astype(BF16)
    ps = pool_scale.reshape(DEPTH, 1, POOL_WIDTH)
    cos_t, sin_t = _rope_tables()
    ck = cache_k.reshape(N_LAT_SEQ, DEPTH, PAST_LEN, KV_WIDTH)
    cv = cache_v.reshape(N_LAT_SEQ, DEPTH, PAST_LEN, KV_WIDTH)

    xs = (x_prompt.reshape(T_CTX, D_MODEL), x_sample.reshape(T_LAT, D_MODEL))
    caches = None
    for l in range(DEPTH):
        side = (_side_job(ffn_w1, (l, 1), W1_SIDE_BLOCKS), _side_job(ffn_w2, (l, 1), W2_SIDE_BLOCKS),
                _side_job(w_out, (l,), WIO_SIDE_BLOCKS))
        (x,), (w1, w2, w_out_b) = _ffn_half(xs, mod_all, norm_w, w1, w2, l, 0, side=side)
        q, kc, vc, kl, vl, a, p = _mixer_proj(
            x, mod_all, norm_w, w_in_b, cos_t, sin_t, ws_cat, bz, l, caches=caches)
        caches = (kc, vc)
        x = _mixer_core(x, mod_all, norm_w, q, kc, vc, a, p, w_out_b, wp_bd, ps, attn_sink, l)
        x = _mixer_core(x, mod_all, norm_w, q, kl, vl, a, p, w_out_b, wp_bd, ps, attn_sink, l,
                        cache=(ck, cv))
        if l < DEPTH - 1:
            side = (_side_job(ffn_w1, (l + 1, 0), W1_SIDE_BLOCKS),
                    _side_job(ffn_w2, (l + 1, 0), W2_SIDE_BLOCKS),
                    _side_job(w_in, (l + 1,), WIO_SIDE_BLOCKS))
            xs, (w1, w2, w_in_b) = _ffn_half((x,), mod_all, norm_w, w1, w2, l, 1, side=side)
        else:
            xs, _ = _ffn_half((x,), mod_all, norm_w, w1, w2, l, 1, split_out=True)
    y_prompt = xs[0].reshape(N_CTX_SEQ, CTX_SEQ, D_MODEL)
    y_sample = xs[1].reshape(N_LAT_SEQ, LAT_SEQ, D_MODEL)
    cache_shape = (N_CTX_SEQ, DEPTH, CTX_SEQ, N_KV_HEADS, HEAD_DIM)
    return y_prompt, y_sample, caches[0].reshape(cache_shape), caches[1].reshape(cache_shape)
```

```python
import functools

import jax
import jax.numpy as jnp
from jax import lax
from jax.experimental import pallas as pl
from jax.experimental.pallas import tpu as pltpu

D_MODEL = 1024
N_CTX_SEQ = 32
CTX_SEQ = 256
DEPTH = 4
N_LAT_SEQ = 2
LAT_SEQ = 2048
PAST_LEN = 512
GRID_W = 64
HEAD_DIM = 64
ATTN_WIDTH = 512
N_HEADS = 8
N_KV_HEADS = 2
KV_WIDTH = 128
WINDOW = 128
GM_WIDTH = 256
GM_HEADS = 4
GM_DIM = 64
CHUNK = 128
POOL_WIDTH = 256
POOL_HALF_WINDOWS = (1, 2, 4, 8)
POOL_DIM = 64
IN_WIDTH = 1536
D_FF = 2816
N_MOD = 9
EPS = 1e-6
ROPE_BASE = 10000.0
NEG_INF = -1e30
ATTN_SCALE = HEAD_DIM ** -0.5

T_CTX = N_CTX_SEQ * CTX_SEQ
T_LAT = N_LAT_SEQ * LAT_SEQ
T_ALL = T_CTX + T_LAT
N_COND = 8

LANES = 128
VMEM_LIMIT = 56 * 1024 * 1024

FFN_TM = 512
FFN_CHUNKS = (512, 512, 512, 512, 512, 256)
W1_SIDE_BLOCKS = 16
W2_SIDE_BLOCKS = 22
WIO_SIDE_BLOCKS = 16
PROJ_TM = 1024
Q_BLK = 256
BAND = Q_BLK + 2 * WINDOW
CORE_CTX_ROWS = 1024
CORE_LAT_ROWS = 512
CORE_CTX_OUT_BLOCKS = 4
LOG2E = 1.4426950408889634
LAT_HEAD_BATCH = 16
CTX_HEAD_BATCH = 16
MOD_TN = 2304

BF16 = jnp.bfloat16
F32 = jnp.float32


def _dot(a, b):
    return jnp.dot(a, b, preferred_element_type=F32)


def _dot_t(a, b):
    return lax.dot_general(a, b, (((1,), (1,)), ((), ())), preferred_element_type=F32)


def _rms(x, g):
    return x * lax.rsqrt(jnp.mean(x * x, axis=-1, keepdims=True) + EPS) * g


def _zero_from(v):
    bits = lax.bitcast_convert_type(v, jnp.uint32)
    return ((bits >> 16) >> 16).astype(F32)


def _cond_row(tile, tm):
    n_ctx = T_CTX // tm
    per_seq = LAT_SEQ // tm
    return jnp.where(tile < n_ctx, 0, 1 + (tile - n_ctx) // per_seq)


def _mod_kernel(c_ref, w_ref, b_ref, o_ref):
    c = c_ref[...]
    s = jax.nn.silu(c).astype(BF16)
    o_ref[...] = _dot(s, w_ref[...].astype(BF16)) + b_ref[...]


def _adaln(cond, w_mod, b_mod):
    n_out = N_MOD * D_MODEL
    return pl.pallas_call(
        _mod_kernel,
        out_shape=jax.ShapeDtypeStruct((DEPTH, N_COND, n_out), F32),
        grid=(DEPTH, n_out // MOD_TN),
        in_specs=[
            pl.BlockSpec((N_COND, D_MODEL), lambda l, n: (0, 0)),
            pl.BlockSpec((None, D_MODEL, MOD_TN), lambda l, n: (l, 0, n)),
            pl.BlockSpec((None, 1, MOD_TN), lambda l, n: (l, 0, n)),
        ],
        out_specs=pl.BlockSpec((None, N_COND, MOD_TN), lambda l, n: (l, 0, n)),
        compiler_params=pltpu.CompilerParams(
            dimension_semantics=("arbitrary", "arbitrary"), vmem_limit_bytes=VMEM_LIMIT),
        name="adaln",
    )(cond, w_mod, b_mod.reshape(DEPTH, 1, n_out))


def _ffn_kernel(*refs, j, pre, post, split_in, split_out, n_side):
    n_x = 2 if split_in else 1
    n_o = 2 if split_out else 1
    xp_refs = refs[:n_x]
    xn_refs = refs[n_x:2 * n_x]
    modp_ref, modn_ref, nw_ref, w1_ref, w2_ref = refs[2 * n_x:2 * n_x + 5]
    pos = 2 * n_x + 5
    side_in = refs[pos:pos + n_side]
    o_refs = refs[pos + n_side:pos + n_side + n_o]
    side_out = refs[pos + n_side + n_o:pos + 2 * n_side + n_o]
    scratch = refs[pos + 2 * n_side + n_o:]
    h_refs, acc_ref = scratch[:2], scratch[2]
    i = pl.program_id(0)
    n_tiles = T_ALL // FFN_TM
    n_ctx = T_CTX // FFN_TM

    def load_x(x_refs, tile, rows):
        if split_in:
            return jnp.where(tile < n_ctx, x_refs[0][rows, :], x_refs[1][rows, :])
        return x_refs[0][rows, :]

    def prologue(x_refs, mod_ref, tile, rows):
        shift = mod_ref[0, j:j + 1, :]
        scale = mod_ref[0, j + 1:j + 2, :]
        x = load_x(x_refs, tile, rows)
        return _rms(x, nw_ref[pre:pre + 1, :] * (1 + scale)) + shift

    def epilogue(rows):
        gate = modp_ref[0, j + 2:j + 3, :]
        y = (load_x(xp_refs, i - 1, rows)
             + _rms(acc_ref[rows, :], nw_ref[post:post + 1, :] * (0.5 * gate)))
        if split_out:
            scratch[3][rows, :] = y
        else:
            o_refs[0][rows, :] = y
        return y

    def store_split():
        prev_ctx = i - 1 < n_ctx

        @pl.when(prev_ctx)
        def _():
            o_refs[0][...] = scratch[3][...]

        @pl.when(jnp.logical_not(prev_ctx))
        def _():
            o_refs[1][...] = scratch[3][...]

    all_rows = slice(0, FFN_TM)
    halves = (slice(0, FFN_TM // 2), slice(FFN_TM // 2, FFN_TM))

    @pl.when(i == 0)
    def _():
        h_refs[0][...] = prologue(xp_refs, modp_ref, 0, all_rows).astype(BF16)
        acc_ref[...] = jnp.zeros_like(acc_ref)

    def main(h_cur, h_nxt):
        def vpu_piece(c):
            if c in (0, 1):
                return _zero_from(epilogue(halves[c])[:, 0:1])
            if c in (2, 3):
                rows = halves[c - 2]
                hf = prologue(xn_refs, modn_ref, i + 1, rows)
                h_nxt[rows, :] = hf.astype(BF16)
                return _zero_from(hf[:, 0:1])
            if c == 4:
                for s_in, s_out in zip(side_in, side_out):
                    s_out[...] = s_in[...].astype(BF16)
            return None

        acc = None
        off = 0
        tie = None
        for c, tk in enumerate(FFN_CHUNKS):
            g = _dot(h_cur[...], w1_ref[:, off:off + tk])
            u = _dot(h_cur[...], w1_ref[:, D_FF + off:D_FF + off + tk])
            if tie is not None:
                tied = halves[(c - 1) % 2]
                u = jnp.concatenate(
                    [u[r, :] + tie if r is tied else u[r, :] for r in halves], axis=0)
            a = (jax.nn.silu(g) * u).astype(BF16)
            part = _dot(a, w2_ref[off:off + tk, :])
            acc = part if acc is None else acc + part
            off += tk
            tie = vpu_piece(c)
        acc_ref[...] = acc

    for par in range(2):
        @pl.when(jnp.logical_and(i < n_tiles, lax.rem(i, 2) == par))
        def _(par=par):
            main(h_refs[par], h_refs[1 - par])

    @pl.when(i == n_tiles)
    def _():
        epilogue(all_rows)

    if split_out:
        store_split()


def _side_job(arr, lead, n_blk):
    rows, cols = arr.shape[-2:]
    blk = rows // n_blk
    nones = (None,) * len(lead)
    in_spec = pl.BlockSpec(nones + (blk, cols), lambda i: lead + (jnp.minimum(i, n_blk - 1), 0))
    out_spec = pl.BlockSpec((blk, cols), lambda i: (jnp.minimum(i, n_blk - 1), 0))
    return arr, in_spec, out_spec, jax.ShapeDtypeStruct((rows, cols), BF16)


def _ffn_half(xs, mod_all, norm_w, w1, w2, l, half, split_out=False, side=()):
    j, pre, post = (0, 0, 1) if half == 0 else (6, 4, 5)
    tm = FFN_TM
    n_tiles = T_ALL // tm
    n_ctx = T_CTX // tm
    n_lat = T_LAT // tm
    split_in = len(xs) == 2
    tile = (tm, D_MODEL)

    def specs(shift):
        t = lambda i: jnp.clip(i + shift, 0, n_tiles - 1)
        if split_in:
            return [pl.BlockSpec(tile, lambda i: (jnp.minimum(t(i), n_ctx - 1), 0)),
                    pl.BlockSpec(tile, lambda i: (jnp.clip(t(i) - n_ctx, 0, n_lat - 1), 0))]
        return [pl.BlockSpec(tile, lambda i: (t(i), 0))]

    def mod_spec(shift):
        t = lambda i: jnp.clip(i + shift, 0, n_tiles - 1)
        return pl.BlockSpec((None, 1, N_MOD, D_MODEL), lambda i: (l, _cond_row(t(i), tm), 0, 0))

    prev = lambda i: jnp.clip(i - 1, 0, n_tiles - 1)
    if split_out:
        out_shape = [jax.ShapeDtypeStruct((T_CTX, D_MODEL), F32),
                     jax.ShapeDtypeStruct((T_LAT, D_MODEL), F32)]
        out_specs = [pl.BlockSpec(tile, lambda i: (jnp.minimum(prev(i), n_ctx - 1), 0)),
                     pl.BlockSpec(tile, lambda i: (jnp.clip(prev(i) - n_ctx, 0, n_lat - 1), 0))]
    else:
        out_shape = [jax.ShapeDtypeStruct((T_ALL, D_MODEL), F32)]
        out_specs = [pl.BlockSpec(tile, lambda i: (prev(i), 0))]
    scratch = [pltpu.VMEM(tile, BF16), pltpu.VMEM(tile, BF16), pltpu.VMEM(tile, F32)]
    if split_out:
        scratch.append(pltpu.VMEM(tile, F32))
    whole = lambda i: (0, 0)
    outs = pl.pallas_call(
        functools.partial(_ffn_kernel, j=j, pre=pre, post=post, split_in=split_in,
                          split_out=split_out, n_side=len(side)),
        out_shape=out_shape + [s[3] for s in side],
        grid=(n_tiles + 1,),
        in_specs=specs(-1) + specs(1) + [
            mod_spec(-1),
            mod_spec(1),
            pl.BlockSpec((None, 6, D_MODEL), lambda i: (l, 0, 0)),
            pl.BlockSpec((D_MODEL, 2 * D_FF), whole, pipeline_mode=pl.Buffered(1)),
            pl.BlockSpec((D_FF, D_MODEL), whole, pipeline_mode=pl.Buffered(1)),
        ] + [s[1] for s in side],
        out_specs=out_specs + [s[2] for s in side],
        scratch_shapes=scratch,
        compiler_params=pltpu.CompilerParams(
            dimension_semantics=("arbitrary",), vmem_limit_bytes=VMEM_LIMIT),
        name="ffn_half",
    )(*xs, *xs, mod_all, mod_all, norm_w, w1, w2, *[s[0] for s in side])
    n_o = len(out_shape)
    return outs[:n_o], outs[n_o:]


def _swap16(x):
    w = x.shape[-1]
    lane = lax.broadcasted_iota(jnp.int32, x.shape, x.ndim - 1)
    first = (lane & 31) < 16
    return jnp.where(first, pltpu.roll(x, w - 16, axis=x.ndim - 1), pltpu.roll(x, 16, axis=x.ndim - 1))


def _rope(x, cos_t, sin_t):
    cols = [x[:, j:j + LANES] for j in range(0, x.shape[-1], LANES)]
    cols = [c * cos_t + _swap16(c) * sin_t for c in cols]
    return cols[0] if len(cols) == 1 else jnp.concatenate(cols, axis=-1)


def _proj_kernel(x_ref, mod_ref, nw_ref, w_ref, cos_ref, sin_ref, ws_ref, bz_ref, *rest):
    q_ref, kc_ref, vc_ref, kl_ref, vl_ref, a_ref, p_ref = rest[-7:]
    i = pl.program_id(0)
    tm = x_ref.shape[0]
    half = tm // 2
    shift = mod_ref[0, 3:4, :]
    scale = mod_ref[0, 4:5, :]
    r = lax.broadcasted_iota(jnp.int32, (GM_WIDTH, GM_WIDTH), 0)
    c = lax.broadcasted_iota(jnp.int32, (GM_WIDTH, GM_WIDTH), 1)
    grp_mean = jnp.where((r >> 6) == (c >> 6), 1.0 / GM_DIM, 0.0).astype(BF16)
    lane = lax.broadcasted_iota(jnp.int32, (CHUNK, GM_WIDTH), 1)
    ws = ws_ref[...]
    bz = bz_ref[...]
    mix_cols = ATTN_WIDTH + 2 * KV_WIDTH

    def pre(rows, after=None):
        x = x_ref[rows, :]
        if after is not None:
            x = x + after
        return _rms(x, nw_ref[2:3, :] * (1 + scale)) + shift

    def gmlp_pool(rows, gu, gv, pool_in):
        p_ref[rows, :] = pool_in
        sq = gv * gv
        sq_hi = sq.astype(BF16)
        sq_lo = (sq - sq_hi.astype(F32)).astype(BF16)
        ms = _dot(sq_hi, grp_mean) + _dot(sq_lo, grp_mean)
        vh = (gv * lax.rsqrt(ms + EPS)).astype(BF16)
        outs = []
        for n in range(half // CHUNK):
            vn = vh[n * CHUNK:(n + 1) * CHUNK, :]
            bd = jnp.concatenate(
                [jnp.where((lane >> 6) == hh, vn, jnp.zeros_like(vn)) for hh in range(GM_HEADS)],
                axis=0)
            outs.append(gu[n * CHUNK:(n + 1) * CHUNK, :] * (_dot(ws, bd) + bz))
        a = jnp.concatenate(outs, axis=0)
        a_ref[rows, :] = a.astype(BF16)
        return a

    rows_a, rows_b = slice(0, half), slice(half, tm)
    h_a = pre(rows_a)
    h_b = pre(rows_b, _zero_from(h_a[:, 0:1]))
    proj_a = _dot(h_a.astype(BF16), w_ref[...])
    gv_col, pool_col = mix_cols + GM_WIDTH, mix_cols + 2 * GM_WIDTH
    a_a = gmlp_pool(rows_a, proj_a[:, mix_cols:gv_col], proj_a[:, gv_col:pool_col],
                    proj_a[:, pool_col:])
    qkv_b = _dot(h_b.astype(BF16), w_ref[:, 0:mix_cols])
    h_b2 = (h_b + _zero_from(a_a[:, 0:1])).astype(BF16)
    mix_b = _dot(h_b2, w_ref[:, mix_cols:])
    gmlp_pool(rows_b, mix_b[:, 0:GM_WIDTH], mix_b[:, GM_WIDTH:2 * GM_WIDTH],
              mix_b[:, 2 * GM_WIDTH:])

    q = [proj_a[:, 0:ATTN_WIDTH] * (ATTN_SCALE * LOG2E), qkv_b[:, 0:ATTN_WIDTH] * (ATTN_SCALE * LOG2E)]
    k = [proj_a[:, ATTN_WIDTH:ATTN_WIDTH + KV_WIDTH], qkv_b[:, ATTN_WIDTH:ATTN_WIDTH + KV_WIDTH]]
    v = [proj_a[:, ATTN_WIDTH + KV_WIDTH:mix_cols], qkv_b[:, ATTN_WIDTH + KV_WIDTH:mix_cols]]
    is_lat = i >= T_CTX // tm

    @pl.when(is_lat)
    def _():
        for n, rows in enumerate((rows_a, rows_b)):
            cos_t = cos_ref[rows, :]
            sin_t = sin_ref[rows, :]
            q_ref[rows, :] = _rope(q[n], cos_t, sin_t).astype(BF16)
            kl_ref[rows, :] = _rope(k[n], cos_t, sin_t)
            vl_ref[rows, :] = v[n]

    @pl.when(jnp.logical_not(is_lat))
    def _():
        for n, rows in enumerate((rows_a, rows_b)):
            q_ref[rows, :] = q[n].astype(BF16)
            for s in range(half // CTX_SEQ):
                seq = n * (half // CTX_SEQ) + s
                kc_ref[seq] = k[n][s * CTX_SEQ:(s + 1) * CTX_SEQ, :]
                vc_ref[seq] = v[n][s * CTX_SEQ:(s + 1) * CTX_SEQ, :]


def _mixer_proj(x, mod_all, norm_w, w_in, cos_t, sin_t, ws_cat, bz, l, caches=None):
    tm = PROJ_TM
    n_ctx = T_CTX // tm
    per_seq = LAT_SEQ // tm
    seq_per_tile = tm // CTX_SEQ

    def tbl_map(i):
        return (lax.rem(jnp.maximum(i - n_ctx, 0), per_seq), 0)

    row = lambda i: (i, 0)
    lat_row = lambda i: (jnp.maximum(i - n_ctx, 0), 0)
    cache_blk = pl.BlockSpec((seq_per_tile, None, CTX_SEQ, KV_WIDTH),
                             lambda i: (jnp.minimum(i, n_ctx - 1), l, 0, 0))
    cache_shape = jax.ShapeDtypeStruct((N_CTX_SEQ, DEPTH, CTX_SEQ, KV_WIDTH), F32)
    extra_specs, extra_args, aliases = [], [], {}
    if caches is not None:
        extra_specs = [pl.BlockSpec(memory_space=pl.ANY)] * 2
        extra_args = list(caches)
        aliases = {8: 1, 9: 2}
    return pl.pallas_call(
        _proj_kernel,
        out_shape=(
            jax.ShapeDtypeStruct((T_ALL, ATTN_WIDTH), BF16),
            cache_shape,
            cache_shape,
            jax.ShapeDtypeStruct((T_LAT, KV_WIDTH), F32),
            jax.ShapeDtypeStruct((T_LAT, KV_WIDTH), F32),
            jax.ShapeDtypeStruct((T_ALL, GM_WIDTH), BF16),
            jax.ShapeDtypeStruct((T_ALL, POOL_WIDTH), F32),
        ),
        grid=(T_ALL // tm,),
        input_output_aliases=aliases,
        in_specs=[
            pl.BlockSpec((tm, D_MODEL), row),
            pl.BlockSpec((None, 1, N_MOD, D_MODEL), lambda i: (l, _cond_row(i, tm), 0, 0)),
            pl.BlockSpec((None, 6, D_MODEL), lambda i: (l, 0, 0)),
            pl.BlockSpec((D_MODEL, IN_WIDTH), lambda i: (0, 0)),
            pl.BlockSpec((tm, LANES), tbl_map),
            pl.BlockSpec((tm, LANES), tbl_map),
            pl.BlockSpec((None, CHUNK, GM_HEADS * CHUNK), lambda i: (l, 0, 0)),
            pl.BlockSpec((None, CHUNK, GM_WIDTH), lambda i: (l, 0, 0)),
        ] + extra_specs,
        out_specs=(
            pl.BlockSpec((tm, ATTN_WIDTH), row),
            cache_blk,
            cache_blk,
            pl.BlockSpec((tm, KV_WIDTH), lat_row),
            pl.BlockSpec((tm, KV_WIDTH), lat_row),
            pl.BlockSpec((tm, GM_WIDTH), row),
            pl.BlockSpec((tm, POOL_WIDTH), row),
        ),
        compiler_params=pltpu.CompilerParams(
            dimension_semantics=("arbitrary",), vmem_limit_bytes=VMEM_LIMIT),
        name="mixer_proj",
    )(x, mod_all, norm_w, w_in, cos_t, sin_t, ws_cat, bz, *extra_args)


def _pool_inv_count(s_len):
    shape = (s_len, POOL_WIDTH)
    row = lax.broadcasted_iota(jnp.int32, shape, 0)
    grp = lax.broadcasted_iota(jnp.int32, shape, 1) >> 6
    half = jnp.where(grp == 0, POOL_HALF_WINDOWS[0],
                     jnp.where(grp == 1, POOL_HALF_WINDOWS[1],
                               jnp.where(grp == 2, POOL_HALF_WINDOWS[2], POOL_HALF_WINDOWS[3])))
    cnt = (jnp.minimum(row + half, s_len) - jnp.maximum(row - half, 0)).astype(F32)
    return 1.0 / cnt


def _pool(x, wp, scale, inv_cnt):
    s_len = x.shape[0]

    def down(y, s):
        row = lax.broadcasted_iota(jnp.int32, y.shape, 0)
        return jnp.where(row >= s, pltpu.roll(y, s, axis=0), 0.0)

    def up(y, s):
        row = lax.broadcasted_iota(jnp.int32, y.shape, 0)
        return jnp.where(row < s_len - s, pltpu.roll(y, s_len - s, axis=0), 0.0)

    first = lax.broadcasted_iota(jnp.int32, (s_len, LANES), 1) < POOL_DIM
    back1 = down(x, 1)
    back2 = back1 + down(back1, 1)
    fwd2 = x + up(x, 1)
    lo = jnp.where(first, (back1 + x)[:, :LANES], (back2 + fwd2)[:, :LANES])
    back4 = back2[:, LANES:] + down(back2[:, LANES:], 2)
    fwd4 = fwd2[:, LANES:] + up(fwd2[:, LANES:], 2)
    back8 = back4 + down(back4, 4)
    fwd8 = fwd4 + up(fwd4, 4)
    hi = jnp.where(first, back4 + fwd4, back8 + fwd8)
    tot = jnp.concatenate([lo, hi], axis=1)
    d = (tot * inv_cnt - x).astype(BF16)
    return _dot(d, wp) * scale


def _dup_halves(t):
    lane = lax.broadcasted_iota(jnp.int32, t.shape, 1)
    lo = lane < HEAD_DIM
    sw = pltpu.roll(t, HEAD_DIM, axis=1)
    return (jnp.where(lo, t, sw).astype(BF16), jnp.where(lo, sw, t).astype(BF16))


def _swap_row_halves(t):
    half = t.shape[0] // 2
    return jnp.concatenate([t[half:], t[:half]], axis=0)


def _head_values(v):
    vt = v.T
    top = lax.broadcasted_iota(jnp.int32, vt.shape, 0) < HEAD_DIM
    sw = _swap_row_halves(vt)
    one = jnp.ones_like(vt)
    pick = lambda a, b: jnp.where(top, a, b).astype(BF16)
    return ((pick(vt, one), pick(one, sw)), (pick(sw, one), pick(one, vt)))


def _attend(q_ref, jobs, sink_ref, l, cat_ref, batch):
    lane = lax.broadcasted_iota(jnp.int32, (Q_BLK, LANES), 1)
    lo = lane < HEAD_DIM
    top = lax.broadcasted_iota(jnp.int32, (LANES, Q_BLK), 0) < HEAD_DIM
    units = [(j, g, p, hf) for g in range(N_KV_HEADS) for p in range(2)
             for j in range(len(jobs)) for hf in range(2)]
    outs, terms = {}, {}
    for b0 in range(0, len(units), batch):
        batch_units = units[b0:b0 + batch]
        sinks = [sink_ref[l, g * 4 + p * 2 + hf] * LOG2E for _, g, p, hf in batch_units]
        scores = []
        for j, g, p, hf in batch_units:
            r0, segs = jobs[j]
            col = (2 * g + p) * LANES
            qpair = q_ref[r0:r0 + Q_BLK, col:col + LANES]
            sel = lo if hf == 0 else jnp.logical_not(lo)
            qm = jnp.where(sel, qpair, jnp.zeros_like(qpair))
            s_h = []
            for keys, _, bias in segs(g):
                s = _dot_t(keys, qm)
                s_h.append(s if bias is None else s + bias)
            scores.append(s_h)
        maxes = []
        for s_h, sink in zip(scores, sinks):
            m = jnp.full((1, Q_BLK), sink, F32)
            for s in s_h:
                m = jnp.maximum(m, jnp.max(s, axis=0, keepdims=True))
            maxes.append(m)
        for (j, g, p, hf), s_h, m, sink in zip(batch_units, scores, maxes, sinks):
            o = None
            for s, (_, vals, _) in zip(s_h, jobs[j][1](g)):
                part = _dot(vals[hf], jnp.exp2(s - m).astype(BF16))
                o = part if o is None else o + part
            outs[j, g, p, hf] = o
            terms[j, g, p, hf] = jnp.exp2(sink - m)
        for j, g, p, hf in batch_units:
            if hf == 0 or (j, g, p, 0) not in outs:
                continue
            o0, o1 = outs.pop((j, g, p, 0)), outs.pop((j, g, p, 1))
            num = jnp.where(top, o0, o1)
            den = jnp.where(top, o1 + terms[j, g, p, 1], o0 + terms[j, g, p, 0])
            r0, col = jobs[j][0], (2 * g + p) * LANES
            cat_ref[r0:r0 + Q_BLK, col:col + LANES] = (num / _swap_row_halves(den)).T.astype(BF16)


def _core_kernel(*refs, latent, l):
    if latent:
        (x_ref, mod_ref, nw_ref, q_ref, k_ref, v_ref, a_ref, p_ref, ck_ref, cv_ref,
         wo_ref, wp_ref, ps_ref, sink_ref, o_ref, cat_ref, pool_ref) = refs
    else:
        (x_ref, mod_ref, nw_ref, q_ref, k_ref, v_ref, a_ref, p_ref,
         wo_ref, wp_ref, ps_ref, sink_ref, o_ref, cat_ref) = refs
    n_rows = x_ref.shape[0]
    pool_cols = slice(ATTN_WIDTH + GM_WIDTH, D_MODEL)

    if latent:
        jq = pl.program_id(1)
        s_len = k_ref.shape[0]

        @pl.when(jq == 0)
        def _():
            pool_ref[...] = _pool(p_ref[...], wp_ref[...], ps_ref[...],
                                  _pool_inv_count(s_len)).astype(BF16)

        ckk = _dup_halves(ck_ref[...])
        cvv = _head_values(cv_ref[...])
        jobs = []
        for s in range(n_rows // Q_BLK):
            q0 = jq * n_rows + s * Q_BLK
            start = pl.multiple_of(jnp.clip(q0 - WINDOW, 0, s_len - BAND), WINDOW)
            kk = _dup_halves(k_ref[pl.ds(start, BAND), :])
            vv = _head_values(v_ref[pl.ds(start, BAND), :])
            kpos = start + lax.broadcasted_iota(jnp.int32, (BAND, Q_BLK), 0)
            qpos = q0 + lax.broadcasted_iota(jnp.int32, (BAND, Q_BLK), 1)
            bias = jnp.where(jnp.abs(qpos - kpos) <= WINDOW, 0.0, NEG_INF)
            jobs.append((s * Q_BLK, lambda g, kk=kk, vv=vv, bias=bias: [
                (kk[g], vv[g], bias), (ckk[g], cvv[g], None)]))
        _attend(q_ref, jobs, sink_ref, l, cat_ref, LAT_HEAD_BATCH)
        cat_ref[:, pool_cols] = pool_ref[pl.ds(pl.multiple_of(jq * n_rows, n_rows), n_rows), :]
    else:
        jobs = []
        inv_cnt = _pool_inv_count(Q_BLK)
        for s in range(n_rows // Q_BLK):
            rows = slice(s * Q_BLK, (s + 1) * Q_BLK)
            kk = _dup_halves(k_ref[s])
            vv = _head_values(v_ref[s])
            cat_ref[rows, pool_cols] = _pool(
                p_ref[rows, :], wp_ref[...], ps_ref[...], inv_cnt).astype(BF16)
            jobs.append((s * Q_BLK, lambda g, kk=kk, vv=vv: [(kk[g], vv[g], None)]))
        _attend(q_ref, jobs, sink_ref, l, cat_ref, CTX_HEAD_BATCH)

    cat_ref[:, ATTN_WIDTH:ATTN_WIDTH + GM_WIDTH] = a_ref[...]
    gain = nw_ref[3:4, :] * mod_ref[0, 5:6, :]
    n_blk = 1 if latent else CORE_CTX_OUT_BLOCKS
    for r in range(n_blk):
        rows = slice(r * (n_rows // n_blk), (r + 1) * (n_rows // n_blk))
        out = _dot(cat_ref[rows, :], wo_ref[...])
        o_ref[rows, :] = x_ref[rows, :] + _rms(out, gain)


def _mixer_core(x, mod_all, norm_w, q, k, v, a, p, w_out, wp_bd, pool_scale, sink, l,
                cache=None):
    latent = cache is not None
    if latent:
        n_rows = CORE_LAT_ROWS
        grid = (N_LAT_SEQ, LAT_SEQ // n_rows)
        blk = lambda b, j: (T_CTX // n_rows + b * grid[1] + j, 0)
        cond = lambda b, j: (l, 1 + b, 0, 0)
        kv_spec = pl.BlockSpec((LAT_SEQ, KV_WIDTH), lambda b, j: (b, 0))
        p_spec = pl.BlockSpec((LAT_SEQ, POOL_WIDTH), lambda b, j: (T_CTX // LAT_SEQ + b, 0))
    else:
        n_rows = CORE_CTX_ROWS
        grid = (T_CTX // n_rows, 1)
        blk = lambda b, j: (b, 0)
        cond = lambda b, j: (l, 0, 0, 0)
        kv_spec = pl.BlockSpec((n_rows // CTX_SEQ, None, CTX_SEQ, KV_WIDTH),
                               lambda b, j: (b, l, 0, 0))
        p_spec = pl.BlockSpec((n_rows, POOL_WIDTH), blk)
    in_specs = [
        pl.BlockSpec((n_rows, D_MODEL), blk),
        pl.BlockSpec((None, 1, N_MOD, D_MODEL), cond),
        pl.BlockSpec((None, 6, D_MODEL), lambda b, j: (l, 0, 0)),
        pl.BlockSpec((n_rows, ATTN_WIDTH), blk),
        kv_spec,
        kv_spec,
        pl.BlockSpec((n_rows, GM_WIDTH), blk),
        p_spec,
    ]
    args = [x, mod_all, norm_w, q, k, v, a, p]
    if latent:
        ck, cv = cache
        in_specs += [
            pl.BlockSpec((None, None, PAST_LEN, KV_WIDTH), lambda b, j: (b, l, 0, 0)),
            pl.BlockSpec((None, None, PAST_LEN, KV_WIDTH), lambda b, j: (b, l, 0, 0)),
        ]
        args += [ck, cv]
    in_specs += [
        pl.BlockSpec((D_MODEL, D_MODEL), lambda b, j: (0, 0)),
        pl.BlockSpec((None, POOL_WIDTH, POOL_WIDTH), lambda b, j: (l, 0, 0)),
        pl.BlockSpec((None, 1, POOL_WIDTH), lambda b, j: (l, 0, 0)),
        pl.BlockSpec(memory_space=pltpu.SMEM),
    ]
    args += [w_out, wp_bd, pool_scale, sink]
    kern = functools.partial(_core_kernel, latent=latent, l=l)
    scratch = [pltpu.VMEM((n_rows, D_MODEL), BF16)]
    if latent:
        scratch.append(pltpu.VMEM((LAT_SEQ, POOL_WIDTH), BF16))
    return pl.pallas_call(
        kern,
        out_shape=jax.ShapeDtypeStruct((T_ALL, D_MODEL), F32),
        grid=grid,
        in_specs=in_specs,
        out_specs=pl.BlockSpec((n_rows, D_MODEL), blk),
        scratch_shapes=scratch,
        input_output_aliases={0: 0},
        compiler_params=pltpu.CompilerParams(
            dimension_semantics=("arbitrary", "arbitrary"), vmem_limit_bytes=VMEM_LIMIT),
        name="mixer_core_lat" if latent else "mixer_core_ctx",
    )(*args)


def _rope_tables():
    t = jnp.arange(LAT_SEQ, dtype=jnp.int32)
    row = (t // GRID_W).astype(F32)
    col = (t % GRID_W).astype(F32)
    nf = HEAD_DIM // 4
    inv = ROPE_BASE ** (-jnp.arange(nf, dtype=F32) / nf)
    ar = row[:, None] * inv
    ac = col[:, None] * inv
    cos64 = jnp.concatenate([jnp.cos(ar), jnp.cos(ar), jnp.cos(ac), jnp.cos(ac)], axis=-1)
    sin64 = jnp.concatenate([-jnp.sin(ar), jnp.sin(ar), -jnp.sin(ac), jnp.sin(ac)], axis=-1)
    return jnp.tile(cos64, (1, 2)), jnp.tile(sin64, (1, 2))


def kernel(x_prompt, x_sample, cache_k, cache_v, c, c_ctx, w_mod, b_mod, norm_w, w_in, w_out,
           attn_sink, w_spatial, b_spatial, w_pool, pool_scale, ffn_w1, ffn_w2):
    cond = jnp.concatenate(
        [c_ctx[None, :], c, jnp.zeros((N_COND - 1 - N_LAT_SEQ, D_MODEL), F32)], axis=0)
    mod_all = _adaln(cond, w_mod, b_mod).reshape(DEPTH, N_COND, N_MOD, D_MODEL)

    w1 = ffn_w1[0, 0].astype(BF16)
    w2 = ffn_w2[0, 0].astype(BF16)
    w_in_b = w_in[0].astype(BF16)
    ws_cat = w_spatial.transpose(0, 2, 1, 3).reshape(DEPTH, CHUNK, GM_HEADS * CHUNK).astype(BF16)
    bz = jnp.repeat(b_spatial.transpose(0, 2, 1), GM_DIM, axis=2)
    eye = jnp.eye(len(POOL_HALF_WINDOWS), dtype=F32)
    wp_bd = jnp.einsum('lgij,gh->lgihj', w_pool, eye).reshape(DEPTH, POOL_WIDTH, POOL_WIDTH).astype(BF16)
    ps = pool_scale.reshape(DEPTH, 1, POOL_WIDTH)
    cos_t, sin_t = _rope_tables()
    ck = cache_k.reshape(N_LAT_SEQ, DEPTH, PAST_LEN, KV_WIDTH)
    cv = cache_v.reshape(N_LAT_SEQ, DEPTH, PAST_LEN, KV_WIDTH)

    xs = (x_prompt.reshape(T_CTX, D_MODEL), x_sample.reshape(T_LAT, D_MODEL))
    caches = None
    for l in range(DEPTH):
        side = (_side_job(ffn_w1, (l, 1), W1_SIDE_BLOCKS), _side_job(ffn_w2, (l, 1), W2_SIDE_BLOCKS),
                _side_job(w_out, (l,), WIO_SIDE_BLOCKS))
        (x,), (w1, w2, w_out_b) = _ffn_half(xs, mod_all, norm_w, w1, w2, l, 0, side=side)
        q, kc, vc, kl, vl, a, p = _mixer_proj(
            x, mod_all, norm_w, w_in_b, cos_t, sin_t, ws_cat, bz, l, caches=caches)
        caches = (kc, vc)
        x = _mixer_core(x, mod_all, norm_w, q, kc, vc, a, p, w_out_b, wp_bd, ps, attn_sink, l)
        x = _mixer_core(x, mod_all, norm_w, q, kl, vl, a, p, w_out_b, wp_bd, ps, attn_sink, l,
                        cache=(ck, cv))
        if l < DEPTH - 1:
            side = (_side_job(ffn_w1, (l + 1, 0), W1_SIDE_BLOCKS),
                    _side_job(ffn_w2, (l + 1, 0), W2_SIDE_BLOCKS),
                    _side_job(w_in, (l + 1,), WIO_SIDE_BLOCKS))
            xs, (w1, w2, w_in_b) = _ffn_half((x,), mod_all, norm_w, w1, w2, l, 1, side=side)
        else:
            xs, _ = _ffn_half((x,), mod_all, norm_w, w1, w2, l, 1, split_out=True)
    y_prompt = xs[0].reshape(N_CTX_SEQ, CTX_SEQ, D_MODEL)
    y_sample = xs[1].reshape(N_LAT_SEQ, LAT_SEQ, D_MODEL)
    cache_shape = (N_CTX_SEQ, DEPTH, CTX_SEQ, N_KV_HEADS, HEAD_DIM)
    return y_prompt, y_sample, caches[0].reshape(cache_shape), caches[1].reshape(cache_shape)
```

```python
import functools

import jax
import jax.numpy as jnp
from jax import lax
from jax.experimental import pallas as pl
from jax.experimental.pallas import tpu as pltpu

D_MODEL = 1024
N_CTX_SEQ = 32
CTX_SEQ = 256
DEPTH = 4
N_LAT_SEQ = 2
LAT_SEQ = 2048
PAST_LEN = 512
GRID_W = 64
HEAD_DIM = 64
ATTN_WIDTH = 512
N_HEADS = 8
N_KV_HEADS = 2
KV_WIDTH = 128
WINDOW = 128
GM_WIDTH = 256
GM_HEADS = 4
GM_DIM = 64
CHUNK = 128
POOL_WIDTH = 256
POOL_HALF_WINDOWS = (1, 2, 4, 8)
POOL_DIM = 64
IN_WIDTH = 1536
D_FF = 2816
N_MOD = 9
EPS = 1e-6
ROPE_BASE = 10000.0
NEG_INF = -1e30
ATTN_SCALE = HEAD_DIM ** -0.5

T_CTX = N_CTX_SEQ * CTX_SEQ
T_LAT = N_LAT_SEQ * LAT_SEQ
T_ALL = T_CTX + T_LAT
N_COND = 8

LANES = 128
VMEM_LIMIT = 56 * 1024 * 1024

FFN_TM = 512
FFN_CHUNKS = (512, 512, 512, 512, 512, 256)
W1_SIDE_BLOCKS = 16
W2_SIDE_BLOCKS = 22
WIO_SIDE_BLOCKS = 16
PROJ_TM = 1024
Q_BLK = 256
BAND = Q_BLK + 2 * WINDOW
CORE_CTX_ROWS = 1024
CORE_LAT_ROWS = 512
LOG2E = 1.4426950408889634
LAT_HEAD_BATCH = 16
CTX_HEAD_BATCH = 16
MOD_TN = 2304

BF16 = jnp.bfloat16
F32 = jnp.float32


def _dot(a, b):
    return jnp.dot(a, b, preferred_element_type=F32)


def _dot_t(a, b):
    return lax.dot_general(a, b, (((1,), (1,)), ((), ())), preferred_element_type=F32)


def _rms(x, g):
    return x * lax.rsqrt(jnp.mean(x * x, axis=-1, keepdims=True) + EPS) * g


def _zero_from(v):
    bits = lax.bitcast_convert_type(v, jnp.uint32)
    return ((bits >> 16) >> 16).astype(F32)


def _cond_row(tile, tm):
    n_ctx = T_CTX // tm
    per_seq = LAT_SEQ // tm
    return jnp.where(tile < n_ctx, 0, 1 + (tile - n_ctx) // per_seq)


def _mod_kernel(c_ref, w_ref, b_ref, o_ref):
    c = c_ref[...]
    s = jax.nn.silu(c).astype(BF16)
    o_ref[...] = _dot(s, w_ref[...].astype(BF16)) + b_ref[...]


def _adaln(cond, w_mod, b_mod):
    n_out = N_MOD * D_MODEL
    return pl.pallas_call(
        _mod_kernel,
        out_shape=jax.ShapeDtypeStruct((DEPTH, N_COND, n_out), F32),
        grid=(DEPTH, n_out // MOD_TN),
        in_specs=[
            pl.BlockSpec((N_COND, D_MODEL), lambda l, n: (0, 0)),
            pl.BlockSpec((None, D_MODEL, MOD_TN), lambda l, n: (l, 0, n)),
            pl.BlockSpec((None, 1, MOD_TN), lambda l, n: (l, 0, n)),
        ],
        out_specs=pl.BlockSpec((None, N_COND, MOD_TN), lambda l, n: (l, 0, n)),
        compiler_params=pltpu.CompilerParams(
            dimension_semantics=("arbitrary", "arbitrary"), vmem_limit_bytes=VMEM_LIMIT),
        name="adaln",
    )(cond, w_mod, b_mod.reshape(DEPTH, 1, n_out))


def _ffn_kernel(*refs, j, pre, post, split_in, split_out, n_side):
    n_x = 2 if split_in else 1
    n_o = 2 if split_out else 1
    xp_refs = refs[:n_x]
    xn_refs = refs[n_x:2 * n_x]
    modp_ref, modn_ref, nw_ref, w1_ref, w2_ref = refs[2 * n_x:2 * n_x + 5]
    pos = 2 * n_x + 5
    side_in = refs[pos:pos + n_side]
    o_refs = refs[pos + n_side:pos + n_side + n_o]
    side_out = refs[pos + n_side + n_o:pos + 2 * n_side + n_o]
    scratch = refs[pos + 2 * n_side + n_o:]
    h_refs, acc_ref = scratch[:2], scratch[2]
    i = pl.program_id(0)
    n_tiles = T_ALL // FFN_TM
    n_ctx = T_CTX // FFN_TM

    def load_x(x_refs, tile, rows):
        if split_in:
            return jnp.where(tile < n_ctx, x_refs[0][rows, :], x_refs[1][rows, :])
        return x_refs[0][rows, :]

    def prologue(x_refs, mod_ref, tile, rows):
        shift = mod_ref[0, j:j + 1, :]
        scale = mod_ref[0, j + 1:j + 2, :]
        x = load_x(x_refs, tile, rows)
        return _rms(x, nw_ref[pre:pre + 1, :] * (1 + scale)) + shift

    def epilogue(rows):
        gate = modp_ref[0, j + 2:j + 3, :]
        y = (load_x(xp_refs, i - 1, rows)
             + _rms(acc_ref[rows, :], nw_ref[post:post + 1, :] * (0.5 * gate)))
        if split_out:
            scratch[3][rows, :] = y
        else:
            o_refs[0][rows, :] = y
        return y

    def store_split():
        prev_ctx = i - 1 < n_ctx

        @pl.when(prev_ctx)
        def _():
            o_refs[0][...] = scratch[3][...]

        @pl.when(jnp.logical_not(prev_ctx))
        def _():
            o_refs[1][...] = scratch[3][...]

    all_rows = slice(0, FFN_TM)
    halves = (slice(0, FFN_TM // 2), slice(FFN_TM // 2, FFN_TM))

    @pl.when(i == 0)
    def _():
        h_refs[0][...] = prologue(xp_refs, modp_ref, 0, all_rows).astype(BF16)
        acc_ref[...] = jnp.zeros_like(acc_ref)

    def main(h_cur, h_nxt):
        def vpu_piece(c):
            if c in (0, 1):
                return _zero_from(epilogue(halves[c])[:, 0:1])
            if c in (2, 3):
                rows = halves[c - 2]
                hf = prologue(xn_refs, modn_ref, i + 1, rows)
                h_nxt[rows, :] = hf.astype(BF16)
                return _zero_from(hf[:, 0:1])
            if c == 4:
                for s_in, s_out in zip(side_in, side_out):
                    s_out[...] = s_in[...].astype(BF16)
            return None

        acc = None
        off = 0
        tie = None
        for c, tk in enumerate(FFN_CHUNKS):
            g = _dot(h_cur[...], w1_ref[:, off:off + tk])
            u = _dot(h_cur[...], w1_ref[:, D_FF + off:D_FF + off + tk])
            if tie is not None:
                tied = halves[(c - 1) % 2]
                u = jnp.concatenate(
                    [u[r, :] + tie if r is tied else u[r, :] for r in halves], axis=0)
            a = (jax.nn.silu(g) * u).astype(BF16)
            part = _dot(a, w2_ref[off:off + tk, :])
            acc = part if acc is None else acc + part
            off += tk
            tie = vpu_piece(c)
        acc_ref[...] = acc

    for par in range(2):
        @pl.when(jnp.logical_and(i < n_tiles, lax.rem(i, 2) == par))
        def _(par=par):
            main(h_refs[par], h_refs[1 - par])

    @pl.when(i == n_tiles)
    def _():
        epilogue(all_rows)

    if split_out:
        store_split()


def _side_job(arr, lead, n_blk):
    rows, cols = arr.shape[-2:]
    blk = rows // n_blk
    nones = (None,) * len(lead)
    in_spec = pl.BlockSpec(nones + (blk, cols), lambda i: lead + (jnp.minimum(i, n_blk - 1), 0))
    out_spec = pl.BlockSpec((blk, cols), lambda i: (jnp.minimum(i, n_blk - 1), 0))
    return arr, in_spec, out_spec, jax.ShapeDtypeStruct((rows, cols), BF16)


def _ffn_half(xs, mod_all, norm_w, w1, w2, l, half, split_out=False, side=()):
    j, pre, post = (0, 0, 1) if half == 0 else (6, 4, 5)
    tm = FFN_TM
    n_tiles = T_ALL // tm
    n_ctx = T_CTX // tm
    n_lat = T_LAT // tm
    split_in = len(xs) == 2
    tile = (tm, D_MODEL)

    def specs(shift):
        t = lambda i: jnp.clip(i + shift, 0, n_tiles - 1)
        if split_in:
            return [pl.BlockSpec(tile, lambda i: (jnp.minimum(t(i), n_ctx - 1), 0)),
                    pl.BlockSpec(tile, lambda i: (jnp.clip(t(i) - n_ctx, 0, n_lat - 1), 0))]
        return [pl.BlockSpec(tile, lambda i: (t(i), 0))]

    def mod_spec(shift):
        t = lambda i: jnp.clip(i + shift, 0, n_tiles - 1)
        return pl.BlockSpec((None, 1, N_MOD, D_MODEL), lambda i: (l, _cond_row(t(i), tm), 0, 0))

    prev = lambda i: jnp.clip(i - 1, 0, n_tiles - 1)
    if split_out:
        out_shape = [jax.ShapeDtypeStruct((T_CTX, D_MODEL), F32),
                     jax.ShapeDtypeStruct((T_LAT, D_MODEL), F32)]
        out_specs = [pl.BlockSpec(tile, lambda i: (jnp.minimum(prev(i), n_ctx - 1), 0)),
                     pl.BlockSpec(tile, lambda i: (jnp.clip(prev(i) - n_ctx, 0, n_lat - 1), 0))]
    else:
        out_shape = [jax.ShapeDtypeStruct((T_ALL, D_MODEL), F32)]
        out_specs = [pl.BlockSpec(tile, lambda i: (prev(i), 0))]
    scratch = [pltpu.VMEM(tile, BF16), pltpu.VMEM(tile, BF16), pltpu.VMEM(tile, F32)]
    if split_out:
        scratch.append(pltpu.VMEM(tile, F32))
    whole = lambda i: (0, 0)
    outs = pl.pallas_call(
        functools.partial(_ffn_kernel, j=j, pre=pre, post=post, split_in=split_in,
                          split_out=split_out, n_side=len(side)),
        out_shape=out_shape + [s[3] for s in side],
        grid=(n_tiles + 1,),
        in_specs=specs(-1) + specs(1) + [
            mod_spec(-1),
            mod_spec(1),
            pl.BlockSpec((None, 6, D_MODEL), lambda i: (l, 0, 0)),
            pl.BlockSpec((D_MODEL, 2 * D_FF), whole, pipeline_mode=pl.Buffered(1)),
            pl.BlockSpec((D_FF, D_MODEL), whole, pipeline_mode=pl.Buffered(1)),
        ] + [s[1] for s in side],
        out_specs=out_specs + [s[2] for s in side],
        scratch_shapes=scratch,
        compiler_params=pltpu.CompilerParams(
            dimension_semantics=("arbitrary",), vmem_limit_bytes=VMEM_LIMIT),
        name="ffn_half",
    )(*xs, *xs, mod_all, mod_all, norm_w, w1, w2, *[s[0] for s in side])
    n_o = len(out_shape)
    return outs[:n_o], outs[n_o:]


def _swap16(x):
    w = x.shape[-1]
    lane = lax.broadcasted_iota(jnp.int32, x.shape, x.ndim - 1)
    first = (lane & 31) < 16
    return jnp.where(first, pltpu.roll(x, w - 16, axis=x.ndim - 1), pltpu.roll(x, 16, axis=x.ndim - 1))


def _rope(x, cos_t, sin_t):
    cols = [x[:, j:j + LANES] for j in range(0, x.shape[-1], LANES)]
    cols = [c * cos_t + _swap16(c) * sin_t for c in cols]
    return cols[0] if len(cols) == 1 else jnp.concatenate(cols, axis=-1)


def _proj_kernel(x_ref, mod_ref, nw_ref, w_ref, cos_ref, sin_ref, ws_ref, bz_ref, *rest):
    q_ref, kc_ref, vc_ref, kl_ref, vl_ref, a_ref, p_ref = rest[-7:]
    i = pl.program_id(0)
    tm = x_ref.shape[0]
    half = tm // 2
    shift = mod_ref[0, 3:4, :]
    scale = mod_ref[0, 4:5, :]
    r = lax.broadcasted_iota(jnp.int32, (GM_WIDTH, GM_WIDTH), 0)
    c = lax.broadcasted_iota(jnp.int32, (GM_WIDTH, GM_WIDTH), 1)
    grp_mean = jnp.where((r >> 6) == (c >> 6), 1.0 / GM_DIM, 0.0).astype(BF16)
    lane = lax.broadcasted_iota(jnp.int32, (CHUNK, GM_WIDTH), 1)
    ws = ws_ref[...]
    bz = bz_ref[...]
    mix_cols = ATTN_WIDTH + 2 * KV_WIDTH

    def pre(rows, after=None):
        x = x_ref[rows, :]
        if after is not None:
            x = x + after
        return _rms(x, nw_ref[2:3, :] * (1 + scale)) + shift

    def gmlp_pool(rows, gu, gv, pool_in):
        p_ref[rows, :] = pool_in
        sq = gv * gv
        sq_hi = sq.astype(BF16)
        sq_lo = (sq - sq_hi.astype(F32)).astype(BF16)
        ms = _dot(sq_hi, grp_mean) + _dot(sq_lo, grp_mean)
        vh = (gv * lax.rsqrt(ms + EPS)).astype(BF16)
        outs = []
        for n in range(half // CHUNK):
            vn = vh[n * CHUNK:(n + 1) * CHUNK, :]
            bd = jnp.concatenate(
                [jnp.where((lane >> 6) == hh, vn, jnp.zeros_like(vn)) for hh in range(GM_HEADS)],
                axis=0)
            outs.append(gu[n * CHUNK:(n + 1) * CHUNK, :] * (_dot(ws, bd) + bz))
        a = jnp.concatenate(outs, axis=0)
        a_ref[rows, :] = a.astype(BF16)
        return a

    rows_a, rows_b = slice(0, half), slice(half, tm)
    h_a = pre(rows_a)
    h_b = pre(rows_b, _zero_from(h_a[:, 0:1]))
    proj_a = _dot(h_a.astype(BF16), w_ref[...])
    gv_col, pool_col = mix_cols + GM_WIDTH, mix_cols + 2 * GM_WIDTH
    a_a = gmlp_pool(rows_a, proj_a[:, mix_cols:gv_col], proj_a[:, gv_col:pool_col],
                    proj_a[:, pool_col:])
    qkv_b = _dot(h_b.astype(BF16), w_ref[:, 0:mix_cols])
    h_b2 = (h_b + _zero_from(a_a[:, 0:1])).astype(BF16)
    mix_b = _dot(h_b2, w_ref[:, mix_cols:])
    gmlp_pool(rows_b, mix_b[:, 0:GM_WIDTH], mix_b[:, GM_WIDTH:2 * GM_WIDTH],
              mix_b[:, 2 * GM_WIDTH:])

    q = [proj_a[:, 0:ATTN_WIDTH] * (ATTN_SCALE * LOG2E), qkv_b[:, 0:ATTN_WIDTH] * (ATTN_SCALE * LOG2E)]
    k = [proj_a[:, ATTN_WIDTH:ATTN_WIDTH + KV_WIDTH], qkv_b[:, ATTN_WIDTH:ATTN_WIDTH + KV_WIDTH]]
    v = [proj_a[:, ATTN_WIDTH + KV_WIDTH:mix_cols], qkv_b[:, ATTN_WIDTH + KV_WIDTH:mix_cols]]
    is_lat = i >= T_CTX // tm

    @pl.when(is_lat)
    def _():
        for n, rows in enumerate((rows_a, rows_b)):
            cos_t = cos_ref[rows, :]
            sin_t = sin_ref[rows, :]
            q_ref[rows, :] = _rope(q[n], cos_t, sin_t).astype(BF16)
            kl_ref[rows, :] = _rope(k[n], cos_t, sin_t)
            vl_ref[rows, :] = v[n]

    @pl.when(jnp.logical_not(is_lat))
    def _():
        for n, rows in enumerate((rows_a, rows_b)):
            q_ref[rows, :] = q[n].astype(BF16)
            for s in range(half // CTX_SEQ):
                seq = n * (half // CTX_SEQ) + s
                kc_ref[seq] = k[n][s * CTX_SEQ:(s + 1) * CTX_SEQ, :]
                vc_ref[seq] = v[n][s * CTX_SEQ:(s + 1) * CTX_SEQ, :]


def _mixer_proj(x, mod_all, norm_w, w_in, cos_t, sin_t, ws_cat, bz, l, caches=None):
    tm = PROJ_TM
    n_ctx = T_CTX // tm
    per_seq = LAT_SEQ // tm
    seq_per_tile = tm // CTX_SEQ

    def tbl_map(i):
        return (lax.rem(jnp.maximum(i - n_ctx, 0), per_seq), 0)

    row = lambda i: (i, 0)
    lat_row = lambda i: (jnp.maximum(i - n_ctx, 0), 0)
    cache_blk = pl.BlockSpec((seq_per_tile, None, CTX_SEQ, KV_WIDTH),
                             lambda i: (jnp.minimum(i, n_ctx - 1), l, 0, 0))
    cache_shape = jax.ShapeDtypeStruct((N_CTX_SEQ, DEPTH, CTX_SEQ, KV_WIDTH), F32)
    extra_specs, extra_args, aliases = [], [], {}
    if caches is not None:
        extra_specs = [pl.BlockSpec(memory_space=pl.ANY)] * 2
        extra_args = list(caches)
        aliases = {8: 1, 9: 2}
    return pl.pallas_call(
        _proj_kernel,
        out_shape=(
            jax.ShapeDtypeStruct((T_ALL, ATTN_WIDTH), BF16),
            cache_shape,
            cache_shape,
            jax.ShapeDtypeStruct((T_LAT, KV_WIDTH), F32),
            jax.ShapeDtypeStruct((T_LAT, KV_WIDTH), F32),
            jax.ShapeDtypeStruct((T_ALL, GM_WIDTH), BF16),
            jax.ShapeDtypeStruct((T_ALL, POOL_WIDTH), F32),
        ),
        grid=(T_ALL // tm,),
        input_output_aliases=aliases,
        in_specs=[
            pl.BlockSpec((tm, D_MODEL), row),
            pl.BlockSpec((None, 1, N_MOD, D_MODEL), lambda i: (l, _cond_row(i, tm), 0, 0)),
            pl.BlockSpec((None, 6, D_MODEL), lambda i: (l, 0, 0)),
            pl.BlockSpec((D_MODEL, IN_WIDTH), lambda i: (0, 0)),
            pl.BlockSpec((tm, LANES), tbl_map),
            pl.BlockSpec((tm, LANES), tbl_map),
            pl.BlockSpec((None, CHUNK, GM_HEADS * CHUNK), lambda i: (l, 0, 0)),
            pl.BlockSpec((None, CHUNK, GM_WIDTH), lambda i: (l, 0, 0)),
        ] + extra_specs,
        out_specs=(
            pl.BlockSpec((tm, ATTN_WIDTH), row),
            cache_blk,
            cache_blk,
            pl.BlockSpec((tm, KV_WIDTH), lat_row),
            pl.BlockSpec((tm, KV_WIDTH), lat_row),
            pl.BlockSpec((tm, GM_WIDTH), row),
            pl.BlockSpec((tm, POOL_WIDTH), row),
        ),
        compiler_params=pltpu.CompilerParams(
            dimension_semantics=("arbitrary",), vmem_limit_bytes=VMEM_LIMIT),
        name="mixer_proj",
    )(x, mod_all, norm_w, w_in, cos_t, sin_t, ws_cat, bz, *extra_args)


def _pool_inv_count(s_len):
    shape = (s_len, POOL_WIDTH)
    row = lax.broadcasted_iota(jnp.int32, shape, 0)
    grp = lax.broadcasted_iota(jnp.int32, shape, 1) >> 6
    half = jnp.where(grp == 0, POOL_HALF_WINDOWS[0],
                     jnp.where(grp == 1, POOL_HALF_WINDOWS[1],
                               jnp.where(grp == 2, POOL_HALF_WINDOWS[2], POOL_HALF_WINDOWS[3])))
    cnt = (jnp.minimum(row + half, s_len) - jnp.maximum(row - half, 0)).astype(F32)
    return 1.0 / cnt


def _pool(x, wp, scale, inv_cnt):
    s_len = x.shape[0]

    def down(y, s):
        row = lax.broadcasted_iota(jnp.int32, y.shape, 0)
        return jnp.where(row >= s, pltpu.roll(y, s, axis=0), 0.0)

    def up(y, s):
        row = lax.broadcasted_iota(jnp.int32, y.shape, 0)
        return jnp.where(row < s_len - s, pltpu.roll(y, s_len - s, axis=0), 0.0)

    first = lax.broadcasted_iota(jnp.int32, (s_len, LANES), 1) < POOL_DIM
    back1 = down(x, 1)
    back2 = back1 + down(back1, 1)
    fwd2 = x + up(x, 1)
    lo = jnp.where(first, (back1 + x)[:, :LANES], (back2 + fwd2)[:, :LANES])
    back4 = back2[:, LANES:] + down(back2[:, LANES:], 2)
    fwd4 = fwd2[:, LANES:] + up(fwd2[:, LANES:], 2)
    back8 = back4 + down(back4, 4)
    fwd8 = fwd4 + up(fwd4, 4)
    hi = jnp.where(first, back4 + fwd4, back8 + fwd8)
    tot = jnp.concatenate([lo, hi], axis=1)
    d = (tot * inv_cnt - x).astype(BF16)
    return _dot(d, wp) * scale


def _dup_halves(t):
    lane = lax.broadcasted_iota(jnp.int32, t.shape, 1)
    lo = lane < HEAD_DIM
    sw = pltpu.roll(t, HEAD_DIM, axis=1)
    return (jnp.where(lo, t, sw).astype(BF16), jnp.where(lo, sw, t).astype(BF16))


def _swap_row_halves(t):
    half = t.shape[0] // 2
    return jnp.concatenate([t[half:], t[:half]], axis=0)


def _head_values(v):
    vt = v.T
    top = lax.broadcasted_iota(jnp.int32, vt.shape, 0) < HEAD_DIM
    sw = _swap_row_halves(vt)
    one = jnp.ones_like(vt)
    pick = lambda a, b: jnp.where(top, a, b).astype(BF16)
    return ((pick(vt, one), pick(one, sw)), (pick(sw, one), pick(one, vt)))


def _attend(q_ref, jobs, sink_ref, l, cat_ref, batch):
    lane = lax.broadcasted_iota(jnp.int32, (Q_BLK, LANES), 1)
    lo = lane < HEAD_DIM
    top = lax.broadcasted_iota(jnp.int32, (LANES, Q_BLK), 0) < HEAD_DIM
    units = [(j, g, p, hf) for g in range(N_KV_HEADS) for p in range(2)
             for j in range(len(jobs)) for hf in range(2)]
    outs, terms = {}, {}
    for b0 in range(0, len(units), batch):
        batch_units = units[b0:b0 + batch]
        sinks = [sink_ref[l, g * 4 + p * 2 + hf] * LOG2E for _, g, p, hf in batch_units]
        scores = []
        for j, g, p, hf in batch_units:
            r0, segs = jobs[j]
            col = (2 * g + p) * LANES
            qpair = q_ref[r0:r0 + Q_BLK, col:col + LANES]
            sel = lo if hf == 0 else jnp.logical_not(lo)
            qm = jnp.where(sel, qpair, jnp.zeros_like(qpair))
            s_h = []
            for keys, _, bias in segs(g):
                s = _dot_t(keys, qm)
                s_h.append(s if bias is None else s + bias)
            scores.append(s_h)
        maxes = []
        for s_h, sink in zip(scores, sinks):
            m = jnp.full((1, Q_BLK), sink, F32)
            for s in s_h:
                m = jnp.maximum(m, jnp.max(s, axis=0, keepdims=True))
            maxes.append(m)
        for (j, g, p, hf), s_h, m, sink in zip(batch_units, scores, maxes, sinks):
            o = None
            for s, (_, vals, _) in zip(s_h, jobs[j][1](g)):
                part = _dot(vals[hf], jnp.exp2(s - m).astype(BF16))
                o = part if o is None else o + part
            outs[j, g, p, hf] = o
            terms[j, g, p, hf] = jnp.exp2(sink - m)
        for j, g, p, hf in batch_units:
            if hf == 0 or (j, g, p, 0) not in outs:
                continue
            o0, o1 = outs.pop((j, g, p, 0)), outs.pop((j, g, p, 1))
            num = jnp.where(top, o0, o1)
            den = jnp.where(top, o1 + terms[j, g, p, 1], o0 + terms[j, g, p, 0])
            r0, col = jobs[j][0], (2 * g + p) * LANES
            cat_ref[r0:r0 + Q_BLK, col:col + LANES] = (num / _swap_row_halves(den)).T.astype(BF16)


def _core_kernel(*refs, latent, l):
    if latent:
        (x_ref, mod_ref, nw_ref, q_ref, k_ref, v_ref, a_ref, p_ref, ck_ref, cv_ref,
         wo_ref, wp_ref, ps_ref, sink_ref, o_ref, cat_ref, pool_ref) = refs
    else:
        (x_ref, mod_ref, nw_ref, q_ref, k_ref, v_ref, a_ref, p_ref,
         wo_ref, wp_ref, ps_ref, sink_ref, o_ref, cat_ref) = refs
    n_rows = x_ref.shape[0]
    pool_cols = slice(ATTN_WIDTH + GM_WIDTH, D_MODEL)

    if latent:
        jq = pl.program_id(1)
        s_len = k_ref.shape[0]

        @pl.when(jq == 0)
        def _():
            pool_ref[...] = _pool(p_ref[...], wp_ref[...], ps_ref[...],
                                  _pool_inv_count(s_len)).astype(BF16)

        ckk = _dup_halves(ck_ref[...])
        cvv = _head_values(cv_ref[...])
        jobs = []
        for s in range(n_rows // Q_BLK):
            q0 = jq * n_rows + s * Q_BLK
            start = pl.multiple_of(jnp.clip(q0 - WINDOW, 0, s_len - BAND), WINDOW)
            kk = _dup_halves(k_ref[pl.ds(start, BAND), :])
            vv = _head_values(v_ref[pl.ds(start, BAND), :])
            kpos = start + lax.broadcasted_iota(jnp.int32, (BAND, Q_BLK), 0)
            qpos = q0 + lax.broadcasted_iota(jnp.int32, (BAND, Q_BLK), 1)
            bias = jnp.where(jnp.abs(qpos - kpos) <= WINDOW, 0.0, NEG_INF)
            jobs.append((s * Q_BLK, lambda g, kk=kk, vv=vv, bias=bias: [
                (kk[g], vv[g], bias), (ckk[g], cvv[g], None)]))
        _attend(q_ref, jobs, sink_ref, l, cat_ref, LAT_HEAD_BATCH)
        cat_ref[:, pool_cols] = pool_ref[pl.ds(pl.multiple_of(jq * n_rows, n_rows), n_rows), :]
    else:
        jobs = []
        inv_cnt = _pool_inv_count(Q_BLK)
        for s in range(n_rows // Q_BLK):
            rows = slice(s * Q_BLK, (s + 1) * Q_BLK)
            kk = _dup_halves(k_ref[s])
            vv = _head_values(v_ref[s])
            cat_ref[rows, pool_cols] = _pool(
                p_ref[rows, :], wp_ref[...], ps_ref[...], inv_cnt).astype(BF16)
            jobs.append((s * Q_BLK, lambda g, kk=kk, vv=vv: [(kk[g], vv[g], None)]))
        _attend(q_ref, jobs, sink_ref, l, cat_ref, CTX_HEAD_BATCH)

    cat_ref[:, ATTN_WIDTH:ATTN_WIDTH + GM_WIDTH] = a_ref[...]
    out = _dot(cat_ref[...], wo_ref[...])
    gate = mod_ref[0, 5:6, :]
    o_ref[...] = x_ref[...] + _rms(out, nw_ref[3:4, :] * gate)


def _mixer_core(x, mod_all, norm_w, q, k, v, a, p, w_out, wp_bd, pool_scale, sink, l,
                cache=None):
    latent = cache is not None
    if latent:
        n_rows = CORE_LAT_ROWS
        grid = (N_LAT_SEQ, LAT_SEQ // n_rows)
        blk = lambda b, j: (T_CTX // n_rows + b * grid[1] + j, 0)
        cond = lambda b, j: (l, 1 + b, 0, 0)
        kv_spec = pl.BlockSpec((LAT_SEQ, KV_WIDTH), lambda b, j: (b, 0))
        p_spec = pl.BlockSpec((LAT_SEQ, POOL_WIDTH), lambda b, j: (T_CTX // LAT_SEQ + b, 0))
    else:
        n_rows = CORE_CTX_ROWS
        grid = (T_CTX // n_rows, 1)
        blk = lambda b, j: (b, 0)
        cond = lambda b, j: (l, 0, 0, 0)
        kv_spec = pl.BlockSpec((n_rows // CTX_SEQ, None, CTX_SEQ, KV_WIDTH),
                               lambda b, j: (b, l, 0, 0))
        p_spec = pl.BlockSpec((n_rows, POOL_WIDTH), blk)
    in_specs = [
        pl.BlockSpec((n_rows, D_MODEL), blk),
        pl.BlockSpec((None, 1, N_MOD, D_MODEL), cond),
        pl.BlockSpec((None, 6, D_MODEL), lambda b, j: (l, 0, 0)),
        pl.BlockSpec((n_rows, ATTN_WIDTH), blk),
        kv_spec,
        kv_spec,
        pl.BlockSpec((n_rows, GM_WIDTH), blk),
        p_spec,
    ]
    args = [x, mod_all, norm_w, q, k, v, a, p]
    if latent:
        ck, cv = cache
        in_specs += [
            pl.BlockSpec((None, None, PAST_LEN, KV_WIDTH), lambda b, j: (b, l, 0, 0)),
            pl.BlockSpec((None, None, PAST_LEN, KV_WIDTH), lambda b, j: (b, l, 0, 0)),
        ]
        args += [ck, cv]
    in_specs += [
        pl.BlockSpec((D_MODEL, D_MODEL), lambda b, j: (0, 0)),
        pl.BlockSpec((None, POOL_WIDTH, POOL_WIDTH), lambda b, j: (l, 0, 0)),
        pl.BlockSpec((None, 1, POOL_WIDTH), lambda b, j: (l, 0, 0)),
        pl.BlockSpec(memory_space=pltpu.SMEM),
    ]
    args += [w_out, wp_bd, pool_scale, sink]
    kern = functools.partial(_core_kernel, latent=latent, l=l)
    scratch = [pltpu.VMEM((n_rows, D_MODEL), BF16)]
    if latent:
        scratch.append(pltpu.VMEM((LAT_SEQ, POOL_WIDTH), BF16))
    return pl.pallas_call(
        kern,
        out_shape=jax.ShapeDtypeStruct((T_ALL, D_MODEL), F32),
        grid=grid,
        in_specs=in_specs,
        out_specs=pl.BlockSpec((n_rows, D_MODEL), blk),
        scratch_shapes=scratch,
        input_output_aliases={0: 0},
        compiler_params=pltpu.CompilerParams(
            dimension_semantics=("arbitrary", "arbitrary"), vmem_limit_bytes=VMEM_LIMIT),
        name="mixer_core_lat" if latent else "mixer_core_ctx",
    )(*args)


def _rope_tables():
    t = jnp.arange(LAT_SEQ, dtype=jnp.int32)
    row = (t // GRID_W).astype(F32)
    col = (t % GRID_W).astype(F32)
    nf = HEAD_DIM // 4
    inv = ROPE_BASE ** (-jnp.arange(nf, dtype=F32) / nf)
    ar = row[:, None] * inv
    ac = col[:, None] * inv
    cos64 = jnp.concatenate([jnp.cos(ar), jnp.cos(ar), jnp.cos(ac), jnp.cos(ac)], axis=-1)
    sin64 = jnp.concatenate([-jnp.sin(ar), jnp.sin(ar), -jnp.sin(ac), jnp.sin(ac)], axis=-1)
    return jnp.tile(cos64, (1, 2)), jnp.tile(sin64, (1, 2))


def kernel(x_prompt, x_sample, cache_k, cache_v, c, c_ctx, w_mod, b_mod, norm_w, w_in, w_out,
           attn_sink, w_spatial, b_spatial, w_pool, pool_scale, ffn_w1, ffn_w2):
    cond = jnp.concatenate(
        [c_ctx[None, :], c, jnp.zeros((N_COND - 1 - N_LAT_SEQ, D_MODEL), F32)], axis=0)
    mod_all = _adaln(cond, w_mod, b_mod).reshape(DEPTH, N_COND, N_MOD, D_MODEL)

    w1 = ffn_w1[0, 0].astype(BF16)
    w2 = ffn_w2[0, 0].astype(BF16)
    w_in_b = w_in[0].astype(BF16)
    ws_cat = w_spatial.transpose(0, 2, 1, 3).reshape(DEPTH, CHUNK, GM_HEADS * CHUNK).astype(BF16)
    bz = jnp.repeat(b_spatial.transpose(0, 2, 1), GM_DIM, axis=2)
    eye = jnp.eye(len(POOL_HALF_WINDOWS), dtype=F32)
    wp_bd = jnp.einsum('lgij,gh->lgihj', w_pool, eye).reshape(DEPTH, POOL_WIDTH, POOL_WIDTH).astype(BF16)
    ps = pool_scale.reshape(DEPTH, 1, POOL_WIDTH)
    cos_t, sin_t = _rope_tables()
    ck = cache_k.reshape(N_LAT_SEQ, DEPTH, PAST_LEN, KV_WIDTH)
    cv = cache_v.reshape(N_LAT_SEQ, DEPTH, PAST_LEN, KV_WIDTH)

    xs = (x_prompt.reshape(T_CTX, D_MODEL), x_sample.reshape(T_LAT, D_MODEL))
    caches = None
    for l in range(DEPTH):
        side = (_side_job(ffn_w1, (l, 1), W1_SIDE_BLOCKS), _side_job(ffn_w2, (l, 1), W2_SIDE_BLOCKS),
                _side_job(w_out, (l,), WIO_SIDE_BLOCKS))
        (x,), (w1, w2, w_out_b) = _ffn_half(xs, mod_all, norm_w, w1, w2, l, 0, side=side)
        q, kc, vc, kl, vl, a, p = _mixer_proj(
            x, mod_all, norm_w, w_in_b, cos_t, sin_t, ws_cat, bz, l, caches=caches)
        caches = (kc, vc)
        x = _mixer_core(x, mod_all, norm_w, q, kc, vc, a, p, w_out_b, wp_bd, ps, attn_sink, l)
        x = _mixer_core(x, mod_all, norm_w, q, kl, vl, a, p, w_out_b, wp_bd, ps, attn_sink, l,
                        cache=(ck, cv))
        if l < DEPTH - 1:
            side = (_side_job(ffn_w1, (l + 1, 0), W1_SIDE_BLOCKS),
                    _side_job(ffn_w2, (l + 1, 0), W2_SIDE_BLOCKS),
                    _side_job(w_in, (l + 1,), WIO_SIDE_BLOCKS))
            xs, (w1, w2, w_in_b) = _ffn_half((x,), mod_all, norm_w, w1, w2, l, 1, side=side)
        else:
            xs, _ = _ffn_half((x,), mod_all, norm_w, w1, w2, l, 1, split_out=True)
    y_prompt = xs[0].reshape(N_CTX_SEQ, CTX_SEQ, D_MODEL)
    y_sample = xs[1].reshape(N_LAT_SEQ, LAT_SEQ, D_MODEL)
    cache_shape = (N_CTX_SEQ, DEPTH, CTX_SEQ, N_KV_HEADS, HEAD_DIM)
    return y_prompt, y_sample, caches[0].reshape(cache_shape), caches[1].reshape(cache_shape)
```
